```python
import math
import jax, jax.numpy as jnp
from jax import lax
import numpy as np

D_MODEL = 2048
BATCH = 16
SEQ = 256
DEPTH = 4
DEC_BATCH = 2
DEC_SEQ = 1024
PAST_LEN = 512

GRID_W = 64
N_HEADS = 4
HEAD_DIM = 128
V_DIM = 2 * HEAD_DIM
ATTN_W = N_HEADS * 2 * HEAD_DIM
CONV_W = 512
CONV_K = 31
SC_W = 512
SC_K = 3
F_W = 512
F_GROUPS = 4
F_GROUP = F_W // F_GROUPS
N_BRANCH = 4
D_IN = 2 * CONV_W + 3 * ATTN_W + 3 * SC_W + F_W
D_FF = ((8 * D_MODEL + 3 * 256 - 1) // (3 * 256)) * 256
ROPE_THETA = 10000.0
EPS = 1e-6
Q_BLOCK = 128

kernel_name = 'hybrid_diffusion_prefix_trunk_step'


def _rmsnorm(x, g):
    xf = x.astype(jnp.float32)
    y = xf * lax.rsqrt(jnp.mean(xf * xf, axis=-1, keepdims=True) + EPS)
    return (y * g.astype(jnp.float32)).astype(x.dtype)


def _layernorm(x, g, b):
    xf = x.astype(jnp.float32)
    mu = jnp.mean(xf, axis=-1, keepdims=True)
    xc = xf - mu
    y = xc * lax.rsqrt(jnp.mean(xc * xc, axis=-1, keepdims=True) + EPS)
    return (y * g.astype(jnp.float32) + b.astype(jnp.float32)).astype(x.dtype)


def _dwconv(x, w):
    pad = w.shape[0] // 2
    return lax.conv_general_dilated(
        x, w[:, None, :].astype(x.dtype), window_strides=(1,), padding=[(pad, pad)],
        dimension_numbers=('NWC', 'WIO', 'NWC'), feature_group_count=x.shape[-1])


def _axial_rope_tables(n_tok):
    rows = n_tok // GRID_W
    t = jnp.arange(rows * GRID_W)
    row = (t // GRID_W).astype(jnp.float32)
    col = (t % GRID_W).astype(jnp.float32)
    half = HEAD_DIM // 2
    inv = ROPE_THETA ** (-jnp.arange(0, half, 2, dtype=jnp.float32) / half)
    ar = row[:, None] * inv
    ac = col[:, None] * inv
    ang = jnp.concatenate([ar, ar, ac, ac], axis=-1)
    return jnp.cos(ang), jnp.sin(ang)


def _rotate_half(x):
    x1, x2 = jnp.split(x, 2, axis=-1)
    return jnp.concatenate([-x2, x1], axis=-1)


def _apply_rope(x, cos, sin):
    half = HEAD_DIM // 2
    rot = jnp.concatenate([_rotate_half(x[..., :half]), _rotate_half(x[..., half:])], axis=-1)
    c = cos[:, None, None, :]
    s = sin[:, None, None, :]
    return (x.astype(jnp.float32) * c + rot.astype(jnp.float32) * s).astype(x.dtype)


def _diff_attention(q, k, v, lam):
    b, tq = q.shape[0], q.shape[1]
    nblk = tq // Q_BLOCK
    qb = jnp.moveaxis(q.reshape(b, nblk, Q_BLOCK, N_HEADS, 2, HEAD_DIM), 1, 0)
    scale = HEAD_DIM ** -0.5

    def blk(qi):
        s = jnp.einsum('bqhmd,bkhmd->bhmqk', qi, k).astype(jnp.float32) * scale
        p = jax.nn.softmax(s, axis=-1)
        a = p[:, :, 0] - lam * p[:, :, 1]
        return jnp.einsum('bhqk,bkhe->bqhe', a.astype(v.dtype), v)

    o = lax.map(blk, qb)
    return jnp.moveaxis(o, 0, 1).reshape(b, tq, N_HEADS, V_DIM)


def _fourier_mix(u):
    b, t = u.shape[0], u.shape[1]
    ug = u.reshape(b, t, F_GROUPS, F_GROUP).astype(jnp.float32)
    f = jnp.fft.fft2(ug, axes=(1, 3), norm='ortho').real
    return f.reshape(b, t, F_W).astype(u.dtype)


def _layer(x, cvec, ctx_k, ctx_v, l, lw):
    bsz, t = x.shape[0], x.shape[1]
    mod = (jax.nn.silu(cvec) @ lw['w_ada'] + lw['b_ada']).reshape(-1, 1, 6 * D_MODEL)
    sh_m, sc_m, g_m, sh_f, sc_f, g_f = jnp.split(mod, 6, axis=-1)

    h = _rmsnorm(x, lw['g_pre_mix']) * (1 + sc_m) + sh_m
    proj = h @ lw['w_in']
    cuts = [2 * CONV_W, 2 * CONV_W + ATTN_W, 2 * CONV_W + 2 * ATTN_W, 2 * CONV_W + 3 * ATTN_W,
            2 * CONV_W + 3 * ATTN_W + SC_W, 2 * CONV_W + 3 * ATTN_W + 2 * SC_W,
            2 * CONV_W + 3 * ATTN_W + 3 * SC_W]
    a_in, q, k, v, s_b, s_c, s_x, d_in = jnp.split(proj, cuts, axis=-1)

    a_val, a_gate = jnp.split(a_in, 2, axis=-1)
    a = _dwconv(a_val * jax.nn.sigmoid(a_gate), lw['conv_a_w']) + lw['conv_a_b']
    a = jax.nn.silu(_layernorm(a, lw['ln_a_g'], lw['ln_a_b']))
    y_a = a @ lw['w_a_out']

    q = q.reshape(bsz, t, N_HEADS, 2, HEAD_DIM)
    k = k.reshape(bsz, t, N_HEADS, 2, HEAD_DIM)
    v = v.reshape(bsz, t, N_HEADS, V_DIM)
    lam_init = 0.8 - 0.6 * math.exp(-0.3 * l)
    lam = (jnp.exp(jnp.sum(lw['lam_q1'].astype(jnp.float32) * lw['lam_k1'].astype(jnp.float32)))
           - jnp.exp(jnp.sum(lw['lam_q2'].astype(jnp.float32) * lw['lam_k2'].astype(jnp.float32)))
           + lam_init)
    if ctx_k is None:
        q_r, k_all, v_all = q, k, v
        new_kv = (k.reshape(bsz, t, N_HEADS, 2 * HEAD_DIM), v)
    else:
        cos, sin = _axial_rope_tables(t)
        q_r = _apply_rope(q, cos, sin)
        k_all = jnp.concatenate([ctx_k.reshape(bsz, -1, N_HEADS, 2, HEAD_DIM), _apply_rope(k, cos, sin)], axis=1)
        v_all = jnp.concatenate([ctx_v, v], axis=1)
        new_kv = None
    o = _diff_attention(q_r, k_all, v_all, lam)
    o = _rmsnorm(o, lw['subln_g']) * (1 - lam_init)
    y_b = o.reshape(bsz, t, ATTN_W) @ lw['w_b_out']

    y_c = (s_b * _dwconv(s_c * s_x, lw['conv_c_w'])) @ lw['w_c_out']

    y_d = _fourier_mix(d_in) @ lw['w_d_out']

    gates = jax.nn.sigmoid(h @ lw['w_gate'] + lw['b_gate']).reshape(bsz, t, N_BRANCH, D_MODEL)
    merged = gates[:, :, 0] * y_a + gates[:, :, 1] * y_b + gates[:, :, 2] * y_c + gates[:, :, 3] * y_d
    mix = merged @ lw['w_o']
    x = x + g_m * _rmsnorm(mix, lw['g_post_mix'])

    h2 = _rmsnorm(x, lw['g_pre_ffn']) * (1 + sc_f) + sh_f
    up, gt = jnp.split(h2 @ lw['w_ffn_up'], 2, axis=-1)
    f = (jax.nn.silu(gt) * up) @ lw['w_ffn_down']
    x = x + g_f * _rmsnorm(f, lw['g_post_ffn'])
    return x, new_kv


def setup_inputs(seed: int = 0) -> dict:
    key = jax.random.key(seed)
    ks = iter(jax.random.split(key, 40))

    def nrm(shape, scale):
        return jax.random.normal(next(ks), shape, jnp.float32) * scale

    def gain(shape):
        return 1.0 + nrm(shape, 0.05)

    L = DEPTH
    return {
        'x_prompt': nrm((BATCH, SEQ, D_MODEL), 1.0),
        'x_sample': nrm((DEC_BATCH, DEC_SEQ, D_MODEL), 1.0),
        'cache_k': nrm((DEC_BATCH, DEPTH, PAST_LEN, N_HEADS, 2 * HEAD_DIM), 1.0),
        'cache_v': nrm((DEC_BATCH, DEPTH, PAST_LEN, N_HEADS, 2 * HEAD_DIM), 1.0),
        'c': nrm((DEC_BATCH, D_MODEL), 1.0),
        'c_ctx': nrm((D_MODEL,), 1.0),
        'w_ada': nrm((L, D_MODEL, 6 * D_MODEL), 0.5 * D_MODEL ** -0.5),
        'b_ada': nrm((L, 6 * D_MODEL), 0.01),
        'g_pre_mix': gain((L, D_MODEL)),
        'g_post_mix': gain((L, D_MODEL)),
        'g_pre_ffn': gain((L, D_MODEL)),
        'g_post_ffn': gain((L, D_MODEL)),
        'w_in': nrm((L, D_MODEL, D_IN), D_MODEL ** -0.5),
        'conv_a_w': nrm((L, CONV_K, CONV_W), CONV_K ** -0.5),
        'conv_a_b': nrm((L, CONV_W), 0.01),
        'ln_a_g': gain((L, CONV_W)),
        'ln_a_b': nrm((L, CONV_W), 0.01),
        'w_a_out': nrm((L, CONV_W, D_MODEL), CONV_W ** -0.5),
        'lam_q1': nrm((L, HEAD_DIM), 0.1),
        'lam_k1': nrm((L, HEAD_DIM), 0.1),
        'lam_q2': nrm((L, HEAD_DIM), 0.1),
        'lam_k2': nrm((L, HEAD_DIM), 0.1),
        'subln_g': gain((L, V_DIM)),
        'w_b_out': nrm((L, ATTN_W, D_MODEL), ATTN_W ** -0.5),
        'conv_c_w': nrm((L, SC_K, SC_W), SC_K ** -0.5),
        'w_c_out': nrm((L, SC_W, D_MODEL), SC_W ** -0.5),
        'w_d_out': nrm((L, F_W, D_MODEL), F_W ** -0.5),
        'w_gate': nrm((L, D_MODEL, N_BRANCH * D_MODEL), D_MODEL ** -0.5),
        'b_gate': nrm((L, N_BRANCH * D_MODEL), 0.01),
        'w_o': nrm((L, D_MODEL, D_MODEL), D_MODEL ** -0.5),
        'w_ffn_up': nrm((L, D_MODEL, 2 * D_FF), D_MODEL ** -0.5),
        'w_ffn_down': nrm((L, D_FF, D_MODEL), D_FF ** -0.5),
    }


def reference(x_prompt, x_sample, cache_k, cache_v, c, c_ctx, w_ada, b_ada, g_pre_mix, g_post_mix,
              g_pre_ffn, g_post_ffn, w_in, conv_a_w, conv_a_b, ln_a_g, ln_a_b, w_a_out,
              lam_q1, lam_k1, lam_q2, lam_k2, subln_g, w_b_out, conv_c_w, w_c_out, w_d_out,
              w_gate, b_gate, w_o, w_ffn_up, w_ffn_down):
    y_prompt = x_prompt
    y_sample = x_sample
    ks_new = []
    vs_new = []
    for l in range(DEPTH):
        lw = {
            'w_ada': w_ada[l], 'b_ada': b_ada[l],
            'g_pre_mix': g_pre_mix[l], 'g_post_mix': g_post_mix[l],
            'g_pre_ffn': g_pre_ffn[l], 'g_post_ffn': g_post_ffn[l],
            'w_in': w_in[l],
            'conv_a_w': conv_a_w[l], 'conv_a_b': conv_a_b[l],
            'ln_a_g': ln_a_g[l], 'ln_a_b': ln_a_b[l], 'w_a_out': w_a_out[l],
            'lam_q1': lam_q1[l], 'lam_k1': lam_k1[l], 'lam_q2': lam_q2[l], 'lam_k2': lam_k2[l],
            'subln_g': subln_g[l], 'w_b_out': w_b_out[l],
            'conv_c_w': conv_c_w[l], 'w_c_out': w_c_out[l],
            'w_d_out': w_d_out[l],
            'w_gate': w_gate[l], 'b_gate': b_gate[l], 'w_o': w_o[l],
            'w_ffn_up': w_ffn_up[l], 'w_ffn_down': w_ffn_down[l],
        }
        y_prompt, kv = _layer(y_prompt, c_ctx, None, None, l, lw)
        ks_new.append(kv[0])
        vs_new.append(kv[1])
        y_sample, _ = _layer(y_sample, c, cache_k[:, l], cache_v[:, l], l, lw)
    new_k = jnp.stack(ks_new, axis=1)
    new_v = jnp.stack(vs_new, axis=1)
    return (y_prompt, y_sample, new_k, new_v)
```

```python
import functools
import math

import numpy as np
import jax
import jax.numpy as jnp
from jax import lax
from jax.experimental import pallas as pl
from jax.experimental.pallas import tpu as pltpu

D_MODEL = 2048
BATCH = 16
SEQ = 256
DEPTH = 4
DEC_BATCH = 2
DEC_SEQ = 1024
PAST_LEN = 512
GRID_W = 64
N_HEADS = 4
HEAD_DIM = 128
V_DIM = 2 * HEAD_DIM
ATTN_W = N_HEADS * 2 * HEAD_DIM
CONV_W = 512
CONV_K = 31
SC_W = 512
SC_K = 3
F_W = 512
F_GROUPS = 4
F_GROUP = F_W // F_GROUPS
N_BRANCH = 4
D_IN = 2 * CONV_W + 3 * ATTN_W + 3 * SC_W + F_W
D_FF = ((8 * D_MODEL + 3 * 256 - 1) // (3 * 256)) * 256
ROPE_THETA = 10000.0
EPS = 1e-6

P_ROWS = BATCH * SEQ
S_ROWS = DEC_BATCH * DEC_SEQ
N_ROWS = P_ROWS + S_ROWS
N_GROUPS = 1 + DEC_BATCH
N_MOD = 6
ADA_ROWS = 8

COL_A = 0
COL_Q = 2 * CONV_W
COL_K = COL_Q + ATTN_W
COL_V = COL_K + ATTN_W
COL_SB = COL_V + ATTN_W
COL_SC = COL_SB + SC_W
COL_SX = COL_SC + SC_W
COL_D = COL_SX + SC_W

SUBLANES = 8
CONV_PAD = 16
CONV_ROWS = 64
VMEM_LIMIT = 56 * 1024 * 1024

F32 = jnp.float32
BF16 = jnp.bfloat16


def _params(n_axes, vmem=VMEM_LIMIT):
    return pltpu.CompilerParams(dimension_semantics=("arbitrary",) * n_axes, vmem_limit_bytes=vmem)


def _group_of_tile(i, tm):
    pt = P_ROWS // tm
    st = DEC_SEQ // tm
    return jnp.maximum(i - (pt - st), 0) // st


def _mod_spec(layer, which, tm, tile_of):
    def index(*ids):
        g = _group_of_tile(tile_of(*ids), tm)
        return (layer * N_GROUPS * N_MOD + g * N_MOD + which, 0, 0)
    return pl.BlockSpec((None, 1, D_MODEL), index)


def _layer_vec_spec(width, layer):
    return pl.BlockSpec((None, 1, width), lambda *ids: (layer, 0, 0))


def _ada_kernel(c_ref, w_ref, b_ref, o_ref):
    s = jax.nn.silu(c_ref[...]).astype(BF16)
    o_ref[...] = jnp.dot(s, w_ref[...].astype(BF16), preferred_element_type=F32) + b_ref[...]


def _ada(cs, w_ada, b_ada):
    tn = 1024
    n = N_MOD * D_MODEL
    return pl.pallas_call(
        _ada_kernel,
        grid=(DEPTH, n // tn),
        in_specs=[
            pl.BlockSpec((ADA_ROWS, D_MODEL), lambda l, j: (0, 0)),
            pl.BlockSpec((None, D_MODEL, tn), lambda l, j: (l, 0, j)),
            pl.BlockSpec((None, 1, tn), lambda l, j: (l, 0, j)),
        ],
        out_specs=pl.BlockSpec((None, ADA_ROWS, tn), lambda l, j: (l, 0, j)),
        out_shape=jax.ShapeDtypeStruct((DEPTH, ADA_ROWS, n), F32),
        compiler_params=_params(2),
        name="ada",
    )(cs, w_ada, b_ada.reshape(DEPTH, 1, n))


def _rms(x):
    return x * lax.rsqrt(jnp.mean(x * x, axis=-1, keepdims=True) + EPS)


def _normmod_kernel(x_ref, g_ref, sc_ref, sh_ref, h_ref):
    y = _rms(x_ref[...]) * g_ref[...]
    h_ref[...] = (y * (1.0 + sc_ref[...]) + sh_ref[...]).astype(BF16)


def _normmod(x, g, mod, layer, which_sc, which_sh):
    tm = 256
    row = pl.BlockSpec((tm, D_MODEL), lambda i: (i, 0))
    tile = lambda i: i
    return pl.pallas_call(
        _normmod_kernel,
        grid=(N_ROWS // tm,),
        in_specs=[row, _layer_vec_spec(D_MODEL, layer),
                  _mod_spec(layer, which_sc, tm, tile), _mod_spec(layer, which_sh, tm, tile)],
        out_specs=row,
        out_shape=jax.ShapeDtypeStruct((N_ROWS, D_MODEL), BF16),
        compiler_params=_params(1),
        name="normmod",
    )(x, g, mod, mod)


def _cast_weight(w_ref, wbf_ref):
    rows = w_ref.shape[0]
    chunk = 256

    def body(r, carry):
        r0 = pl.multiple_of(r * chunk, chunk)
        wbf_ref[pl.ds(r0, chunk), :] = w_ref[pl.ds(r0, chunk), :].astype(BF16)
        return carry

    lax.fori_loop(0, rows // chunk, body, 0)


def _mm_kernel(x_ref, w_ref, *rest, sigmoid_bias):
    if sigmoid_bias:
        b_ref, o_ref, wbf_ref = rest
    else:
        o_ref, wbf_ref = rest

    @pl.when(pl.program_id(1) == 0)
    def _():
        _cast_weight(w_ref, wbf_ref)

    acc = jnp.dot(x_ref[...], wbf_ref[...], preferred_element_type=F32)
    if sigmoid_bias:
        acc = jax.nn.sigmoid(acc + b_ref[...])
    o_ref[...] = acc.astype(o_ref.dtype)


def _mm(x, w, layer, *, tm, tn, out_dtype, bias=None, name):
    m, k = x.shape
    n = w.shape[2]
    in_specs = [pl.BlockSpec((tm, k), lambda j, i: (i, 0)),
                pl.BlockSpec((None, k, tn), lambda j, i: (layer, 0, j))]
    args = [x, w]
    if bias is not None:
        in_specs.append(pl.BlockSpec((None, 1, tn), lambda j, i: (layer, 0, j)))
        args.append(bias)
    return pl.pallas_call(
        functools.partial(_mm_kernel, sigmoid_bias=bias is not None),
        grid=(n // tn, m // tm),
        in_specs=in_specs,
        out_specs=pl.BlockSpec((tm, tn), lambda j, i: (i, j)),
        out_shape=jax.ShapeDtypeStruct((m, n), out_dtype),
        scratch_shapes=[pltpu.VMEM((k, tn), BF16)],
        compiler_params=_params(2),
        name=name,
    )(*args)


def _merge_kernel(a_ref, b_ref, c_ref, d_ref, ga_ref, gb_ref, gc_ref, gd_ref,
                  wa_ref, wb_ref, wc_ref, wd_ref, o_ref, wa_s, wb_s, wc_s, wd_s):
    @pl.when(pl.program_id(1) == 0)
    def _():
        _cast_weight(wa_ref, wa_s)
        _cast_weight(wb_ref, wb_s)
        _cast_weight(wc_ref, wc_s)
        _cast_weight(wd_ref, wd_s)

    acc = ga_ref[...].astype(F32) * jnp.dot(a_ref[...], wa_s[...], preferred_element_type=F32)
    acc += gb_ref[...].astype(F32) * jnp.dot(b_ref[...], wb_s[...], preferred_element_type=F32)
    acc += gc_ref[...].astype(F32) * jnp.dot(c_ref[...], wc_s[...], preferred_element_type=F32)
    acc += gd_ref[...].astype(F32) * jnp.dot(d_ref[...], wd_s[...], preferred_element_type=F32)
    o_ref[...] = acc.astype(BF16)


def _merge(ya, yb, yc, yd, gates, w_a, w_b, w_c, w_d, layer):
    tm, tn = 512, 1024
    nt = D_MODEL // tn

    def br(width):
        return pl.BlockSpec((tm, width), lambda j, i: (i, 0))

    def gate(branch):
        return pl.BlockSpec((tm, tn), lambda j, i: (i, branch * nt + j))

    def wt(width):
        return pl.BlockSpec((None, width, tn), lambda j, i: (layer, 0, j))

    return pl.pallas_call(
        _merge_kernel,
        grid=(nt, N_ROWS // tm),
        in_specs=[br(CONV_W), br(ATTN_W), br(SC_W), br(F_W),
                  gate(0), gate(1), gate(2), gate(3),
                  wt(CONV_W), wt(ATTN_W), wt(SC_W), wt(F_W)],
        out_specs=pl.BlockSpec((tm, tn), lambda j, i: (i, j)),
        out_shape=jax.ShapeDtypeStruct((N_ROWS, D_MODEL), BF16),
        scratch_shapes=[pltpu.VMEM((CONV_W, tn), BF16), pltpu.VMEM((ATTN_W, tn), BF16),
                        pltpu.VMEM((SC_W, tn), BF16), pltpu.VMEM((F_W, tn), BF16)],
        compiler_params=_params(2),
        name="merge",
    )(ya, yb, yc, yd, gates, gates, gates, gates, w_a, w_b, w_c, w_d)


def _ffn_up_kernel(x_ref, wu_ref, wg_ref, o_ref, wu_s, wg_s):
    @pl.when(pl.program_id(1) == 0)
    def _():
        _cast_weight(wu_ref, wu_s)
        _cast_weight(wg_ref, wg_s)

    x = x_ref[...]
    up = jnp.dot(x, wu_s[...], preferred_element_type=F32)
    gt = jnp.dot(x, wg_s[...], preferred_element_type=F32)
    o_ref[...] = (jax.nn.silu(gt) * up).astype(BF16)


def _ffn_up(h, w_up, layer):
    tm, tn = 512, 512
    nt = D_FF // tn
    return pl.pallas_call(
        _ffn_up_kernel,
        grid=(nt, N_ROWS // tm),
        in_specs=[pl.BlockSpec((tm, D_MODEL), lambda j, i: (i, 0)),
                  pl.BlockSpec((None, D_MODEL, tn), lambda j, i: (layer, 0, j)),
                  pl.BlockSpec((None, D_MODEL, tn), lambda j, i: (layer, 0, nt + j))],
        out_specs=pl.BlockSpec((tm, tn), lambda j, i: (i, j)),
        out_shape=jax.ShapeDtypeStruct((N_ROWS, D_FF), BF16),
        scratch_shapes=[pltpu.VMEM((D_MODEL, tn), BF16), pltpu.VMEM((D_MODEL, tn), BF16)],
        compiler_params=_params(2),
        name="ffn_up",
    )(h, w_up, w_up)


EPI_ROWS = 16


def _mm_resid_kernel(*refs, nw, nt, chunk, tm, with_next):
    if with_next:
        (a_ref, w_ref, x_ref, gate_ref, gpost_ref, gpre_ref, sc_ref, sh_ref,
         xo_ref, h_ref, wbf_ref, acc0, acc1) = refs
    else:
        a_ref, w_ref, x_ref, gate_ref, gpost_ref, xo_ref, wbf_ref, acc0, acc1 = refs
    accs = (acc0, acc1)
    s = pl.program_id(0)
    t = s - nw

    @pl.when(s < nw)
    def _():
        r0 = pl.multiple_of(s * chunk, chunk)
        wbf_ref[pl.ds(r0, chunk), :] = w_ref[...].astype(BF16)

    def matmul(acc):
        acc[...] = jnp.dot(a_ref[...], wbf_ref[...], preferred_element_type=F32)

    def epilogue(acc):
        gg = gate_ref[...] * gpost_ref[...]
        if with_next:
            pp = gpre_ref[...] * (1.0 + sc_ref[...])
            sh = sh_ref[...]
        for r in range(0, tm, EPI_ROWS):
            rows = slice(r, r + EPI_ROWS)
            xn = x_ref[rows, :] + _rms(acc[rows, :]) * gg
            xo_ref[rows, :] = xn
            if with_next:
                h_ref[rows, :] = (_rms(xn) * pp + sh).astype(BF16)

    @pl.when(t == 0)
    def _():
        matmul(accs[0])

    for parity in range(2):
        @pl.when((t >= 1) & (t < nt) & (t % 2 == parity))
        def _():
            matmul(accs[parity])
            epilogue(accs[1 - parity])

    @pl.when(t == nt)
    def _():
        epilogue(accs[(nt - 1) % 2])


def _mm_resid(a, w, x, mod, layer, which_gate, g_post, nxt, *, tm, chunk, name):
    k = a.shape[1]
    nw = k // chunk
    nt = N_ROWS // tm
    mm_tile = lambda s: jnp.clip(s - nw, 0, nt - 1)
    epi_tile = lambda s: jnp.clip(s - nw - 1, 0, nt - 1)
    row = pl.BlockSpec((tm, D_MODEL), lambda s: (epi_tile(s), 0))
    in_specs = [pl.BlockSpec((tm, k), lambda s: (mm_tile(s), 0)),
                pl.BlockSpec((None, chunk, D_MODEL), lambda s: (layer, jnp.minimum(s, nw - 1), 0)),
                row, _mod_spec(layer, which_gate, tm, epi_tile), _layer_vec_spec(D_MODEL, layer)]
    args = [a, w, x, mod, g_post]
    out_specs = [row]
    out_shape = [jax.ShapeDtypeStruct((N_ROWS, D_MODEL), F32)]
    if nxt is not None:
        g_pre, nl, wsc, wsh = nxt
        in_specs += [_layer_vec_spec(D_MODEL, nl), _mod_spec(nl, wsc, tm, epi_tile),
                     _mod_spec(nl, wsh, tm, epi_tile)]
        args += [g_pre, mod, mod]
        out_specs.append(row)
        out_shape.append(jax.ShapeDtypeStruct((N_ROWS, D_MODEL), BF16))
    return pl.pallas_call(
        functools.partial(_mm_resid_kernel, nw=nw, nt=nt, chunk=chunk, tm=tm, with_next=nxt is not None),
        grid=(nw + nt + 1,),
        in_specs=in_specs,
        out_specs=out_specs,
        out_shape=out_shape,
        scratch_shapes=[pltpu.VMEM((k, D_MODEL), BF16),
                        pltpu.VMEM((tm, D_MODEL), F32), pltpu.VMEM((tm, D_MODEL), F32)],
        compiler_params=_params(1),
        name=name,
    )(*args)


def _seq_spec(t, width, col, row0):
    return pl.BlockSpec((t, width), lambda b: (row0 // t + b, col // width))


def _seq_call(kernel_fn, name, t, nseq, row0, in_specs, args, width, dst, scratch):
    aliases = {}
    if dst is not None:
        in_specs = in_specs + [pl.BlockSpec(memory_space=pl.ANY)]
        args = args + [dst]
        aliases = {len(args) - 1: 0}
    return pl.pallas_call(
        kernel_fn,
        grid=(nseq,),
        in_specs=in_specs,
        out_specs=pl.BlockSpec((t, width), lambda b: (row0 // t + b, 0)),
        out_shape=jax.ShapeDtypeStruct((N_ROWS, width), BF16),
        scratch_shapes=scratch,
        input_output_aliases=aliases,
        compiler_params=_params(1),
        name=name,
    )(*args)


def _dwconv_chunk(pad_ref, w_ref, r0, ksize):
    win = CONV_ROWS + 2 * CONV_PAD
    window = pad_ref[r0:r0 + win, :]
    acc = None
    for b in range(SUBLANES):
        taps = [j for j in range(ksize) if (CONV_PAD - ksize // 2 + j) % SUBLANES == b]
        if not taps:
            continue
        shifted = window if b == 0 else pltpu.roll(window, win - b, 0)
        for j in taps:
            a0 = (CONV_PAD - ksize // 2 + j) - b
            term = w_ref[j:j + 1, :] * shifted[a0:a0 + CONV_ROWS, :]
            acc = term if acc is None else acc + term
    return acc


def _fill_pad(pad_ref, u, t):
    zeros = jnp.zeros((CONV_PAD, pad_ref.shape[1]), F32)
    pad_ref[0:CONV_PAD, :] = zeros
    pad_ref[CONV_PAD + t:2 * CONV_PAD + t, :] = zeros
    pad_ref[CONV_PAD:CONV_PAD + t, :] = u


def _conv_a_kernel(a_ref, w_ref, cb_ref, lg_ref, lb_ref, *rest, t):
    o_ref, pad_ref = rest[-2:]
    _fill_pad(pad_ref, a_ref[:, 0:CONV_W] * jax.nn.sigmoid(a_ref[:, CONV_W:2 * CONV_W]), t)
    for r0 in range(0, t, CONV_ROWS):
        acc = _dwconv_chunk(pad_ref, w_ref, r0, CONV_K) + cb_ref[...]
        mu = jnp.mean(acc, axis=-1, keepdims=True)
        xc = acc - mu
        y = xc * lax.rsqrt(jnp.mean(xc * xc, axis=-1, keepdims=True) + EPS)
        y = y * lg_ref[...] + lb_ref[...]
        o_ref[r0:r0 + CONV_ROWS, :] = jax.nn.silu(y).astype(BF16)


def _conv_a(proj, conv_w, conv_b, ln_g, ln_b, layer, t, nseq, row0, dst):
    in_specs = [_seq_spec(t, 2 * CONV_W, COL_A, row0),
                pl.BlockSpec((None, CONV_K, CONV_W), lambda b: (layer, 0, 0)),
                _layer_vec_spec(CONV_W, layer), _layer_vec_spec(CONV_W, layer),
                _layer_vec_spec(CONV_W, layer)]
    return _seq_call(functools.partial(_conv_a_kernel, t=t), "conv_a", t, nseq, row0, in_specs,
                     [proj, conv_w, conv_b, ln_g, ln_b], CONV_W, dst,
                     [pltpu.VMEM((t + 2 * CONV_PAD, CONV_W), F32)])


def _conv_c_kernel(sb_ref, sc_ref, sx_ref, w_ref, *rest, t):
    o_ref, pad_ref = rest[-2:]
    _fill_pad(pad_ref, sc_ref[...] * sx_ref[...], t)
    for r0 in range(0, t, CONV_ROWS):
        acc = _dwconv_chunk(pad_ref, w_ref, r0, SC_K)
        o_ref[r0:r0 + CONV_ROWS, :] = (sb_ref[r0:r0 + CONV_ROWS, :] * acc).astype(BF16)


def _conv_c(proj, conv_w, layer, t, nseq, row0, dst):
    in_specs = [_seq_spec(t, SC_W, COL_SB, row0), _seq_spec(t, SC_W, COL_SC, row0),
                _seq_spec(t, SC_W, COL_SX, row0),
                pl.BlockSpec((None, SC_K, SC_W), lambda b: (layer, 0, 0))]
    return _seq_call(functools.partial(_conv_c_kernel, t=t), "conv_c", t, nseq, row0, in_specs,
                     [proj, proj, proj, conv_w], SC_W, dst,
                     [pltpu.VMEM((t + 2 * CONV_PAD, SC_W), F32)])


def _fourier_kernel(u_ref, cc_ref, sc_ref, ct_ref, st_ref, *rest):
    o_ref = rest[-1]
    u = u_ref[...].astype(BF16)
    yc = jnp.dot(u, cc_ref[...], preferred_element_type=F32).astype(BF16)
    ys = jnp.dot(u, sc_ref[...], preferred_element_type=F32).astype(BF16)
    f = (jnp.dot(ct_ref[...], yc, preferred_element_type=F32)
         - jnp.dot(st_ref[...], ys, preferred_element_type=F32))
    o_ref[...] = f.astype(BF16)


def _dft_tables(n):
    idx = np.arange(n)
    ang = 2.0 * np.pi * ((idx[:, None] * idx[None, :]) % n) / n
    s = 1.0 / math.sqrt(n)
    return (np.cos(ang) * s).astype(np.float32), (np.sin(ang) * s).astype(np.float32)


def _fourier_tables():
    cg, sg = _dft_tables(F_GROUP)
    eye = np.eye(F_GROUPS, dtype=np.float32)
    tabs = {"chan": (np.kron(eye, cg), np.kron(eye, sg)), SEQ: _dft_tables(SEQ), DEC_SEQ: _dft_tables(DEC_SEQ)}
    return {k: tuple(jnp.asarray(m).astype(BF16) for m in v) for k, v in tabs.items()}


def _fourier(proj, tabs, t, nseq, row0, dst):
    full = lambda n: pl.BlockSpec((n, n), lambda b: (0, 0))
    in_specs = [_seq_spec(t, F_W, COL_D, row0), full(F_W), full(F_W), full(t), full(t)]
    return _seq_call(_fourier_kernel, "fourier", t, nseq, row0, in_specs,
                     [proj, *tabs["chan"], *tabs[t]], F_W, dst, [])


def _lam_of(lq1_ref, lk1_ref, lq2_ref, lk2_ref, lam_init):
    s1 = jnp.sum(lq1_ref[...] * lk1_ref[...], axis=-1, keepdims=True)
    s2 = jnp.sum(lq2_ref[...] * lk2_ref[...], axis=-1, keepdims=True)
    return jnp.exp(s1) - jnp.exp(s2) + lam_init


def _softmax(s):
    e = jnp.exp(s - jnp.max(s, axis=-1, keepdims=True))
    return e / jnp.sum(e, axis=-1, keepdims=True)


def _diff_attn_heads(q, k_of, v_of, lam, subln, lam_init, o_ref):
    scale = HEAD_DIM ** -0.5
    nt = (((1,), (1,)), ((), ()))
    for h in range(N_HEADS):
        maps = []
        for m in range(2):
            c = 2 * h + m
            qc = q[:, c * HEAD_DIM:(c + 1) * HEAD_DIM].astype(BF16)
            s = lax.dot_general(qc, k_of(c), nt, preferred_element_type=F32) * scale
            maps.append(_softmax(s))
        a = (maps[0] - lam * maps[1]).astype(BF16)
        o = jnp.dot(a, v_of(h), preferred_element_type=F32)
        o = _rms(o) * subln * (1.0 - lam_init)
        o_ref[:, h * V_DIM:(h + 1) * V_DIM] = o.astype(BF16)


def _attn_ctx_kernel(q_ref, k_ref, v_ref, lq1, lk1, lq2, lk2, sg_ref, *rest, lam_init):
    o_ref, nk_ref, nv_ref = rest[-3:]
    lam = _lam_of(lq1, lk1, lq2, lk2, lam_init)
    k = k_ref[...]
    v = v_ref[...]
    nk_ref[...] = k
    nv_ref[...] = v
    kb = k.astype(BF16)
    vb = v.astype(BF16)
    _diff_attn_heads(q_ref[...],
                     lambda c: kb[:, c * HEAD_DIM:(c + 1) * HEAD_DIM],
                     lambda h: vb[:, h * V_DIM:(h + 1) * V_DIM],
                     lam, sg_ref[...], lam_init, o_ref)


def _attn_ctx(proj, lam_q1, lam_k1, lam_q2, lam_k2, subln_g, layer, new_kv):
    lam_init = 0.8 - 0.6 * math.exp(-0.3 * layer)
    lv = _layer_vec_spec(HEAD_DIM, layer)
    in_specs = [_seq_spec(SEQ, ATTN_W, COL_Q, 0), _seq_spec(SEQ, ATTN_W, COL_K, 0),
                _seq_spec(SEQ, ATTN_W, COL_V, 0), lv, lv, lv, lv, _layer_vec_spec(V_DIM, layer)]
    args = [proj, proj, proj, lam_q1, lam_k1, lam_q2, lam_k2, subln_g]
    aliases = {}
    if new_kv is not None:
        in_specs += [pl.BlockSpec(memory_space=pl.ANY)] * 2
        aliases = {len(args): 1, len(args) + 1: 2}
        args += list(new_kv)
    kv_spec = pl.BlockSpec((None, None, SEQ, ATTN_W), lambda b: (b, layer, 0, 0))
    kv_shape = jax.ShapeDtypeStruct((BATCH, DEPTH, SEQ, ATTN_W), F32)
    yb, nk, nv = pl.pallas_call(
        functools.partial(_attn_ctx_kernel, lam_init=lam_init),
        grid=(BATCH,),
        in_specs=in_specs,
        out_specs=[pl.BlockSpec((SEQ, ATTN_W), lambda b: (b, 0)), kv_spec, kv_spec],
        out_shape=[jax.ShapeDtypeStruct((N_ROWS, ATTN_W), BF16), kv_shape, kv_shape],
        input_output_aliases=aliases,
        compiler_params=_params(1),
        name="attn_ctx",
    )(*args)
    return yb, (nk, nv)


def _rope(x, cos, sin_a, sin_b):
    outs = []
    for c in range(ATTN_W // HEAD_DIM):
        xc = x[:, c * HEAD_DIM:(c + 1) * HEAD_DIM]
        fwd = pltpu.roll(xc, HEAD_DIM - HEAD_DIM // 4, 1)
        bwd = pltpu.roll(xc, HEAD_DIM // 4, 1)
        outs.append(xc * cos + fwd * sin_a + bwd * sin_b)
    return outs


def _attn_lat_kernel(q_ref, k_ref, v_ref, ck_ref, cv_ref, cosq_ref, sinaq_ref, sinbq_ref,
                     cos_ref, sina_ref, sinb_ref, lq1, lk1, lq2, lk2, sg_ref, dst_ref, o_ref,
                     kall_ref, vall_ref, *, lam_init):
    @pl.when(pl.program_id(1) == 0)
    def _():
        kall_ref[0:PAST_LEN, :] = ck_ref[...].astype(BF16)
        vall_ref[0:PAST_LEN, :] = cv_ref[...].astype(BF16)
        vall_ref[PAST_LEN:PAST_LEN + DEC_SEQ, :] = v_ref[...].astype(BF16)
        kr = _rope(k_ref[...], cos_ref[...], sina_ref[...], sinb_ref[...])
        for c, kc in enumerate(kr):
            kall_ref[PAST_LEN:PAST_LEN + DEC_SEQ, c * HEAD_DIM:(c + 1) * HEAD_DIM] = kc.astype(BF16)

    lam = _lam_of(lq1, lk1, lq2, lk2, lam_init)
    q = jnp.concatenate(_rope(q_ref[...], cosq_ref[...], sinaq_ref[...], sinbq_ref[...]), axis=-1)
    _diff_attn_heads(q,
                     lambda c: kall_ref[:, c * HEAD_DIM:(c + 1) * HEAD_DIM],
                     lambda h: vall_ref[:, h * V_DIM:(h + 1) * V_DIM],
                     lam, sg_ref[...], lam_init, o_ref)


def _rope_tables():
    t = jnp.arange(DEC_SEQ)
    row = (t // GRID_W).astype(F32)
    col = (t % GRID_W).astype(F32)
    half = HEAD_DIM // 2
    inv = ROPE_THETA ** (-jnp.arange(0, half, 2, dtype=F32) / half)
    ar = row[:, None] * inv
    ac = col[:, None] * inv
    ang = jnp.concatenate([ar, ar, ac, ac], axis=-1)
    cos, sin = jnp.cos(ang), jnp.sin(ang)
    first = (np.arange(HEAD_DIM) % half) < (half // 2)
    sin_a = jnp.where(first, -sin, 0.0)
    sin_b = jnp.where(first, 0.0, sin)
    return cos, sin_a, sin_b


def _attn_lat(proj, cache_k, cache_v, rope, lam_q1, lam_k1, lam_q2, lam_k2, subln_g, layer, dst):
    lam_init = 0.8 - 0.6 * math.exp(-0.3 * layer)
    tq = 256
    nq = DEC_SEQ // tq
    lv = _layer_vec_spec(HEAD_DIM, layer)
    qtab = pl.BlockSpec((tq, HEAD_DIM), lambda b, i: (i, 0))
    ktab = pl.BlockSpec((DEC_SEQ, HEAD_DIM), lambda b, i: (0, 0))
    cache = pl.BlockSpec((None, None, PAST_LEN, ATTN_W), lambda b, i: (b, layer, 0, 0))

    def kv(col):
        return pl.BlockSpec((DEC_SEQ, ATTN_W), lambda b, i: (P_ROWS // DEC_SEQ + b, col // ATTN_W))

    cos, sin_a, sin_b = rope
    args = [proj, proj, proj, cache_k, cache_v, cos, sin_a, sin_b, cos, sin_a, sin_b,
            lam_q1, lam_k1, lam_q2, lam_k2, subln_g, dst]
    return pl.pallas_call(
        functools.partial(_attn_lat_kernel, lam_init=lam_init),
        grid=(DEC_BATCH, nq),
        in_specs=[pl.BlockSpec((tq, ATTN_W), lambda b, i: (P_ROWS // tq + b * nq + i, COL_Q // ATTN_W)),
                  kv(COL_K), kv(COL_V), cache, cache, qtab, qtab, qtab, ktab, ktab, ktab,
                  lv, lv, lv, lv, _layer_vec_spec(V_DIM, layer), pl.BlockSpec(memory_space=pl.ANY)],
        out_specs=pl.BlockSpec((tq, ATTN_W), lambda b, i: (P_ROWS // tq + b * nq + i, 0)),
        out_shape=jax.ShapeDtypeStruct((N_ROWS, ATTN_W), BF16),
        scratch_shapes=[pltpu.VMEM((PAST_LEN + DEC_SEQ, ATTN_W), BF16),
                        pltpu.VMEM((PAST_LEN + DEC_SEQ, ATTN_W), BF16)],
        input_output_aliases={len(args) - 1: 0},
        compiler_params=_params(2),
        name="attn_lat",
    )(*args)


def kernel(x_prompt, x_sample, cache_k, cache_v, c, c_ctx, w_ada, b_ada, g_pre_mix, g_post_mix,
           g_pre_ffn, g_post_ffn, w_in, conv_a_w, conv_a_b, ln_a_g, ln_a_b, w_a_out,
           lam_q1, lam_k1, lam_q2, lam_k2, subln_g, w_b_out, conv_c_w, w_c_out, w_d_out,
           w_gate, b_gate, w_o, w_ffn_up, w_ffn_down):
    x = jnp.concatenate([x_prompt.reshape(P_ROWS, D_MODEL), x_sample.reshape(S_ROWS, D_MODEL)], axis=0)
    cs = jnp.concatenate([c_ctx[None, :], c, jnp.zeros((ADA_ROWS - N_GROUPS, D_MODEL), F32)], axis=0)
    mod = _ada(cs, w_ada, b_ada)[:, :N_GROUPS].reshape(DEPTH * N_GROUPS * N_MOD, 1, D_MODEL)

    vec = lambda a: a.reshape(DEPTH, 1, a.shape[-1])
    g_pre_mix, g_post_mix, g_pre_ffn, g_post_ffn = map(vec, (g_pre_mix, g_post_mix, g_pre_ffn, g_post_ffn))
    conv_a_b, ln_a_g, ln_a_b, subln_g, b_gate = map(vec, (conv_a_b, ln_a_g, ln_a_b, subln_g, b_gate))
    lam_q1, lam_k1, lam_q2, lam_k2 = map(vec, (lam_q1, lam_k1, lam_q2, lam_k2))
    cache_k = cache_k.reshape(DEC_BATCH, DEPTH, PAST_LEN, ATTN_W)
    cache_v = cache_v.reshape(DEC_BATCH, DEPTH, PAST_LEN, ATTN_W)
    rope = _rope_tables()
    dft = _fourier_tables()

    def both(fn, *args):
        return fn(*args, DEC_SEQ, DEC_BATCH, P_ROWS, fn(*args, SEQ, BATCH, 0, None))

    h = _normmod(x, g_pre_mix, mod, 0, 1, 0)
    new_kv = None
    for l in range(DEPTH):
        proj = _mm(h, w_in, l, tm=512, tn=1024, out_dtype=F32, name="mm_in")
        gates = _mm(h, w_gate, l, tm=512, tn=1024, out_dtype=BF16, bias=b_gate, name="mm_gate")

        ya = both(_conv_a, proj, conv_a_w, conv_a_b, ln_a_g, ln_a_b, l)
        yc = both(_conv_c, proj, conv_c_w, l)
        yd = both(_fourier, proj, dft)
        lam_args = (lam_q1, lam_k1, lam_q2, lam_k2, subln_g, l)
        yb, new_kv = _attn_ctx(proj, *lam_args, new_kv)
        yb = _attn_lat(proj, cache_k, cache_v, rope, *lam_args, yb)

        merged = _merge(ya, yb, yc, yd, gates, w_a_out, w_b_out, w_c_out, w_d_out, l)
        x, h2 = _mm_resid(merged, w_o, x, mod, l, 2, g_post_mix, (g_pre_ffn, l, 4, 3),
                          tm=512, chunk=256, name="mm_o")
        act = _ffn_up(h2, w_ffn_up, l)
        nxt = (g_pre_mix, l + 1, 1, 0) if l + 1 < DEPTH else None
        res = _mm_resid(act, w_ffn_down, x, mod, l, 5, g_post_ffn, nxt, tm=256, chunk=256, name="mm_down")
        x, h = res if nxt is not None else (res[0], None)

    y_prompt = x[:P_ROWS].reshape(BATCH, SEQ, D_MODEL)
    y_sample = x[P_ROWS:].reshape(DEC_BATCH, DEC_SEQ, D_MODEL)
    new_k, new_v = (a.reshape(BATCH, DEPTH, SEQ, N_HEADS, V_DIM) for a in new_kv)
    return y_prompt, y_sample, new_k, new_v
```

```python
import functools
import math

import numpy as np
import jax
import jax.numpy as jnp
from jax import lax
from jax.experimental import pallas as pl
from jax.experimental.pallas import tpu as pltpu

D_MODEL = 2048
BATCH = 16
SEQ = 256
DEPTH = 4
DEC_BATCH = 2
DEC_SEQ = 1024
PAST_LEN = 512
GRID_W = 64
N_HEADS = 4
HEAD_DIM = 128
V_DIM = 2 * HEAD_DIM
ATTN_W = N_HEADS * 2 * HEAD_DIM
CONV_W = 512
CONV_K = 31
SC_W = 512
SC_K = 3
F_W = 512
F_GROUPS = 4
F_GROUP = F_W // F_GROUPS
N_BRANCH = 4
D_IN = 2 * CONV_W + 3 * ATTN_W + 3 * SC_W + F_W
D_FF = ((8 * D_MODEL + 3 * 256 - 1) // (3 * 256)) * 256
ROPE_THETA = 10000.0
EPS = 1e-6

P_ROWS = BATCH * SEQ
S_ROWS = DEC_BATCH * DEC_SEQ
N_ROWS = P_ROWS + S_ROWS
N_GROUPS = 1 + DEC_BATCH
N_MOD = 6
ADA_ROWS = 8

COL_A = 0
COL_Q = 2 * CONV_W
COL_K = COL_Q + ATTN_W
COL_V = COL_K + ATTN_W
COL_SB = COL_V + ATTN_W
COL_SC = COL_SB + SC_W
COL_SX = COL_SC + SC_W
COL_D = COL_SX + SC_W

SUBLANES = 8
CONV_PAD = 16
CONV_ROWS = 64
VMEM_LIMIT = 56 * 1024 * 1024
MM_PIECE = 256

F32 = jnp.float32
BF16 = jnp.bfloat16


def _params(n_axes, vmem=VMEM_LIMIT):
    return pltpu.CompilerParams(dimension_semantics=("arbitrary",) * n_axes, vmem_limit_bytes=vmem)


def _group_of_tile(i, tm):
    pt = P_ROWS // tm
    st = DEC_SEQ // tm
    return jnp.maximum(i - (pt - st), 0) // st


def _mod_spec(layer, which, tm, tile_of):
    def index(*ids):
        g = _group_of_tile(tile_of(*ids), tm)
        return (layer * N_GROUPS * N_MOD + g * N_MOD + which, 0, 0)
    return pl.BlockSpec((None, 1, D_MODEL), index)


def _layer_vec_spec(width, layer):
    return pl.BlockSpec((None, 1, width), lambda *ids: (layer, 0, 0))


def _ada_kernel(c_ref, w_ref, b_ref, o_ref):
    s = jax.nn.silu(c_ref[...]).astype(BF16)
    o_ref[...] = jnp.dot(s, w_ref[...].astype(BF16), preferred_element_type=F32) + b_ref[...]


def _ada(cs, w_ada, b_ada):
    tn = 1024
    n = N_MOD * D_MODEL
    return pl.pallas_call(
        _ada_kernel,
        grid=(DEPTH, n // tn),
        in_specs=[
            pl.BlockSpec((ADA_ROWS, D_MODEL), lambda l, j: (0, 0)),
            pl.BlockSpec((None, D_MODEL, tn), lambda l, j: (l, 0, j)),
            pl.BlockSpec((None, 1, tn), lambda l, j: (l, 0, j)),
        ],
        out_specs=pl.BlockSpec((None, ADA_ROWS, tn), lambda l, j: (l, 0, j)),
        out_shape=jax.ShapeDtypeStruct((DEPTH, ADA_ROWS, n), F32),
        compiler_params=_params(2),
        name="ada",
    )(cs, w_ada, b_ada.reshape(DEPTH, 1, n))


def _rms(x):
    return x * lax.rsqrt(jnp.mean(x * x, axis=-1, keepdims=True) + EPS)


def _normmod_kernel(x_ref, g_ref, sc_ref, sh_ref, h_ref):
    y = _rms(x_ref[...]) * g_ref[...]
    h_ref[...] = (y * (1.0 + sc_ref[...]) + sh_ref[...]).astype(BF16)


def _normmod(x, g, mod, layer, which_sc, which_sh):
    tm = 256
    row = pl.BlockSpec((tm, D_MODEL), lambda i: (i, 0))
    tile = lambda i: i
    return pl.pallas_call(
        _normmod_kernel,
        grid=(N_ROWS // tm,),
        in_specs=[row, _layer_vec_spec(D_MODEL, layer),
                  _mod_spec(layer, which_sc, tm, tile), _mod_spec(layer, which_sh, tm, tile)],
        out_specs=row,
        out_shape=jax.ShapeDtypeStruct((N_ROWS, D_MODEL), BF16),
        compiler_params=_params(1),
        name="normmod",
    )(x, g, mod, mod)


def _cast_weight(w_ref, wbf_ref):
    rows = w_ref.shape[0]
    chunk = 256

    def body(r, carry):
        r0 = pl.multiple_of(r * chunk, chunk)
        wbf_ref[pl.ds(r0, chunk), :] = w_ref[pl.ds(r0, chunk), :].astype(BF16)
        return carry

    lax.fori_loop(0, rows // chunk, body, 0)


def _mm_kernel(x_ref, w_ref, *rest, sigmoid_bias):
    if sigmoid_bias:
        b_ref, o_ref, wbf_ref = rest
    else:
        o_ref, wbf_ref = rest

    @pl.when(pl.program_id(1) == 0)
    def _():
        _cast_weight(w_ref, wbf_ref)

    x = x_ref[...]
    for c in range(0, o_ref.shape[1], MM_PIECE):
        cols = slice(c, c + MM_PIECE)
        acc = jnp.dot(x, wbf_ref[:, cols], preferred_element_type=F32)
        if sigmoid_bias:
            acc = jax.nn.sigmoid(acc + b_ref[:, cols])
        o_ref[:, cols] = acc.astype(o_ref.dtype)


def _mm(x, w, layer, *, tm, tn, out_dtype, bias=None, name):
    m, k = x.shape
    n = w.shape[2]
    in_specs = [pl.BlockSpec((tm, k), lambda j, i: (i, 0)),
                pl.BlockSpec((None, k, tn), lambda j, i: (layer, 0, j))]
    args = [x, w]
    if bias is not None:
        in_specs.append(pl.BlockSpec((None, 1, tn), lambda j, i: (layer, 0, j)))
        args.append(bias)
    return pl.pallas_call(
        functools.partial(_mm_kernel, sigmoid_bias=bias is not None),
        grid=(n // tn, m // tm),
        in_specs=in_specs,
        out_specs=pl.BlockSpec((tm, tn), lambda j, i: (i, j)),
        out_shape=jax.ShapeDtypeStruct((m, n), out_dtype),
        scratch_shapes=[pltpu.VMEM((k, tn), BF16)],
        compiler_params=_params(2),
        name=name,
    )(*args)


def _merge_kernel(a_ref, b_ref, c_ref, d_ref, ga_ref, gb_ref, gc_ref, gd_ref,
                  wa_ref, wb_ref, wc_ref, wd_ref, o_ref, wa_s, wb_s, wc_s, wd_s):
    @pl.when(pl.program_id(1) == 0)
    def _():
        _cast_weight(wa_ref, wa_s)
        _cast_weight(wb_ref, wb_s)
        _cast_weight(wc_ref, wc_s)
        _cast_weight(wd_ref, wd_s)

    acc = ga_ref[...].astype(F32) * jnp.dot(a_ref[...], wa_s[...], preferred_element_type=F32)
    acc += gb_ref[...].astype(F32) * jnp.dot(b_ref[...], wb_s[...], preferred_element_type=F32)
    acc += gc_ref[...].astype(F32) * jnp.dot(c_ref[...], wc_s[...], preferred_element_type=F32)
    acc += gd_ref[...].astype(F32) * jnp.dot(d_ref[...], wd_s[...], preferred_element_type=F32)
    o_ref[...] = acc.astype(BF16)


def _merge(ya, yb, yc, yd, gates, w_a, w_b, w_c, w_d, layer):
    tm, tn = 512, 1024
    nt = D_MODEL // tn

    def br(width):
        return pl.BlockSpec((tm, width), lambda j, i: (i, 0))

    def gate(branch):
        return pl.BlockSpec((tm, tn), lambda j, i: (i, branch * nt + j))

    def wt(width):
        return pl.BlockSpec((None, width, tn), lambda j, i: (layer, 0, j))

    return pl.pallas_call(
        _merge_kernel,
        grid=(nt, N_ROWS // tm),
        in_specs=[br(CONV_W), br(ATTN_W), br(SC_W), br(F_W),
                  gate(0), gate(1), gate(2), gate(3),
                  wt(CONV_W), wt(ATTN_W), wt(SC_W), wt(F_W)],
        out_specs=pl.BlockSpec((tm, tn), lambda j, i: (i, j)),
        out_shape=jax.ShapeDtypeStruct((N_ROWS, D_MODEL), BF16),
        scratch_shapes=[pltpu.VMEM((CONV_W, tn), BF16), pltpu.VMEM((ATTN_W, tn), BF16),
                        pltpu.VMEM((SC_W, tn), BF16), pltpu.VMEM((F_W, tn), BF16)],
        compiler_params=_params(2),
        name="merge",
    )(ya, yb, yc, yd, gates, gates, gates, gates, w_a, w_b, w_c, w_d)


def _ffn_up_kernel(x_ref, wu_ref, wg_ref, o_ref, wu_s, wg_s):
    @pl.when(pl.program_id(1) == 0)
    def _():
        _cast_weight(wu_ref, wu_s)
        _cast_weight(wg_ref, wg_s)

    x = x_ref[...]
    for c in range(0, o_ref.shape[1], MM_PIECE):
        cols = slice(c, c + MM_PIECE)
        up = jnp.dot(x, wu_s[:, cols], preferred_element_type=F32)
        gt = jnp.dot(x, wg_s[:, cols], preferred_element_type=F32)
        o_ref[:, cols] = (jax.nn.silu(gt) * up).astype(BF16)


def _ffn_up(h, w_up, layer):
    tm, tn = 1024, 512
    nt = D_FF // tn
    return pl.pallas_call(
        _ffn_up_kernel,
        grid=(nt, N_ROWS // tm),
        in_specs=[pl.BlockSpec((tm, D_MODEL), lambda j, i: (i, 0)),
                  pl.BlockSpec((None, D_MODEL, tn), lambda j, i: (layer, 0, j)),
                  pl.BlockSpec((None, D_MODEL, tn), lambda j, i: (layer, 0, nt + j))],
        out_specs=pl.BlockSpec((tm, tn), lambda j, i: (i, j)),
        out_shape=jax.ShapeDtypeStruct((N_ROWS, D_FF), BF16),
        scratch_shapes=[pltpu.VMEM((D_MODEL, tn), BF16), pltpu.VMEM((D_MODEL, tn), BF16)],
        compiler_params=_params(2),
        name="ffn_up",
    )(h, w_up, w_up)


EPI_ROWS = 16


def _mm_resid_kernel(*refs, nw, chunk, tm, piece, with_next):
    if with_next:
        a_ref, w_ref, x_ref, gate_ref, gpost_ref, gpre_ref, sc_ref, sh_ref, xo_ref, h_ref, wbf_ref = refs
    else:
        a_ref, w_ref, x_ref, gate_ref, gpost_ref, xo_ref, wbf_ref = refs
    s = pl.program_id(0)

    @pl.when(s < nw)
    def _():
        r0 = pl.multiple_of(s * chunk, chunk)
        wbf_ref[pl.ds(r0, chunk), :] = w_ref[...].astype(BF16)

    @pl.when(s >= nw)
    def _():
        gg = gate_ref[...] * gpost_ref[...]
        if with_next:
            pp = gpre_ref[...] * (1.0 + sc_ref[...])
            sh = sh_ref[...]
        for p0 in range(0, tm, piece):
            f = jnp.dot(a_ref[p0:p0 + piece, :], wbf_ref[...], preferred_element_type=F32)
            for r in range(0, piece, EPI_ROWS):
                rows = slice(p0 + r, p0 + r + EPI_ROWS)
                xn = x_ref[rows, :] + _rms(f[r:r + EPI_ROWS, :]) * gg
                xo_ref[rows, :] = xn
                if with_next:
                    h_ref[rows, :] = (_rms(xn) * pp + sh).astype(BF16)


def _mm_resid(a, w, x, mod, layer, which_gate, g_post, nxt, *, tm, piece, chunk, name):
    k = a.shape[1]
    nw = k // chunk
    nt = N_ROWS // tm
    tile = lambda s: jnp.maximum(s - nw, 0)
    row = pl.BlockSpec((tm, D_MODEL), lambda s: (tile(s), 0))
    in_specs = [pl.BlockSpec((tm, k), lambda s: (tile(s), 0)),
                pl.BlockSpec((None, chunk, D_MODEL), lambda s: (layer, jnp.minimum(s, nw - 1), 0)),
                row, _mod_spec(layer, which_gate, tm, tile), _layer_vec_spec(D_MODEL, layer)]
    args = [a, w, x, mod, g_post]
    out_specs = [row]
    out_shape = [jax.ShapeDtypeStruct((N_ROWS, D_MODEL), F32)]
    if nxt is not None:
        g_pre, nl, wsc, wsh = nxt
        in_specs += [_layer_vec_spec(D_MODEL, nl), _mod_spec(nl, wsc, tm, tile), _mod_spec(nl, wsh, tm, tile)]
        args += [g_pre, mod, mod]
        out_specs.append(row)
        out_shape.append(jax.ShapeDtypeStruct((N_ROWS, D_MODEL), BF16))
    return pl.pallas_call(
        functools.partial(_mm_resid_kernel, nw=nw, chunk=chunk, tm=tm, piece=piece, with_next=nxt is not None),
        grid=(nw + nt,),
        in_specs=in_specs,
        out_specs=out_specs,
        out_shape=out_shape,
        scratch_shapes=[pltpu.VMEM((k, D_MODEL), BF16)],
        compiler_params=_params(1),
        name=name,
    )(*args)


def _seq_spec(t, width, col, row0):
    return pl.BlockSpec((t, width), lambda b: (row0 // t + b, col // width))


def _seq_call(kernel_fn, name, t, nseq, row0, in_specs, args, width, dst, scratch):
    aliases = {}
    if dst is not None:
        in_specs = in_specs + [pl.BlockSpec(memory_space=pl.ANY)]
        args = args + [dst]
        aliases = {len(args) - 1: 0}
    return pl.pallas_call(
        kernel_fn,
        grid=(nseq,),
        in_specs=in_specs,
        out_specs=pl.BlockSpec((t, width), lambda b: (row0 // t + b, 0)),
        out_shape=jax.ShapeDtypeStruct((N_ROWS, width), BF16),
        scratch_shapes=scratch,
        input_output_aliases=aliases,
        compiler_params=_params(1),
        name=name,
    )(*args)


def _dwconv_chunk(pad_ref, w_ref, r0, ksize):
    win = CONV_ROWS + 2 * CONV_PAD
    window = pad_ref[r0:r0 + win, :]
    acc = None
    for b in range(SUBLANES):
        taps = [j for j in range(ksize) if (CONV_PAD - ksize // 2 + j) % SUBLANES == b]
        if not taps:
            continue
        shifted = window if b == 0 else pltpu.roll(window, win - b, 0)
        for j in taps:
            a0 = (CONV_PAD - ksize // 2 + j) - b
            term = w_ref[j:j + 1, :] * shifted[a0:a0 + CONV_ROWS, :]
            acc = term if acc is None else acc + term
    return acc


def _fill_pad(pad_ref, u, t):
    zeros = jnp.zeros((CONV_PAD, pad_ref.shape[1]), F32)
    pad_ref[0:CONV_PAD, :] = zeros
    pad_ref[CONV_PAD + t:2 * CONV_PAD + t, :] = zeros
    pad_ref[CONV_PAD:CONV_PAD + t, :] = u


def _conv_a_kernel(a_ref, w_ref, cb_ref, lg_ref, lb_ref, *rest, t):
    o_ref, pad_ref = rest[-2:]
    _fill_pad(pad_ref, a_ref[:, 0:CONV_W] * jax.nn.sigmoid(a_ref[:, CONV_W:2 * CONV_W]), t)
    for r0 in range(0, t, CONV_ROWS):
        acc = _dwconv_chunk(pad_ref, w_ref, r0, CONV_K) + cb_ref[...]
        mu = jnp.mean(acc, axis=-1, keepdims=True)
        xc = acc - mu
        y = xc * lax.rsqrt(jnp.mean(xc * xc, axis=-1, keepdims=True) + EPS)
        y = y * lg_ref[...] + lb_ref[...]
        o_ref[r0:r0 + CONV_ROWS, :] = jax.nn.silu(y).astype(BF16)


def _conv_a(proj, conv_w, conv_b, ln_g, ln_b, layer, t, nseq, row0, dst):
    in_specs = [_seq_spec(t, 2 * CONV_W, COL_A, row0),
                pl.BlockSpec((None, CONV_K, CONV_W), lambda b: (layer, 0, 0)),
                _layer_vec_spec(CONV_W, layer), _layer_vec_spec(CONV_W, layer),
                _layer_vec_spec(CONV_W, layer)]
    return _seq_call(functools.partial(_conv_a_kernel, t=t), "conv_a", t, nseq, row0, in_specs,
                     [proj, conv_w, conv_b, ln_g, ln_b], CONV_W, dst,
                     [pltpu.VMEM((t + 2 * CONV_PAD, CONV_W), F32)])


def _conv_c_kernel(sb_ref, sc_ref, sx_ref, w_ref, *rest, t):
    o_ref, pad_ref = rest[-2:]
    _fill_pad(pad_ref, sc_ref[...] * sx_ref[...], t)
    for r0 in range(0, t, CONV_ROWS):
        acc = _dwconv_chunk(pad_ref, w_ref, r0, SC_K)
        o_ref[r0:r0 + CONV_ROWS, :] = (sb_ref[r0:r0 + CONV_ROWS, :] * acc).astype(BF16)


def _conv_c(proj, conv_w, layer, t, nseq, row0, dst):
    in_specs = [_seq_spec(t, SC_W, COL_SB, row0), _seq_spec(t, SC_W, COL_SC, row0),
                _seq_spec(t, SC_W, COL_SX, row0),
                pl.BlockSpec((None, SC_K, SC_W), lambda b: (layer, 0, 0))]
    return _seq_call(functools.partial(_conv_c_kernel, t=t), "conv_c", t, nseq, row0, in_specs,
                     [proj, proj, proj, conv_w], SC_W, dst,
                     [pltpu.VMEM((t + 2 * CONV_PAD, SC_W), F32)])


def _fourier_kernel(u_ref, cc_ref, sc_ref, ct_ref, st_ref, *rest):
    o_ref = rest[-1]
    u = u_ref[...].astype(BF16)
    yc = jnp.dot(u, cc_ref[...], preferred_element_type=F32).astype(BF16)
    ys = jnp.dot(u, sc_ref[...], preferred_element_type=F32).astype(BF16)
    f = (jnp.dot(ct_ref[...], yc, preferred_element_type=F32)
         - jnp.dot(st_ref[...], ys, preferred_element_type=F32))
    o_ref[...] = f.astype(BF16)


def _dft_tables(n):
    idx = np.arange(n)
    ang = 2.0 * np.pi * ((idx[:, None] * idx[None, :]) % n) / n
    s = 1.0 / math.sqrt(n)
    return (np.cos(ang) * s).astype(np.float32), (np.sin(ang) * s).astype(np.float32)


def _fourier_tables():
    cg, sg = _dft_tables(F_GROUP)
    eye = np.eye(F_GROUPS, dtype=np.float32)
    tabs = {"chan": (np.kron(eye, cg), np.kron(eye, sg)), SEQ: _dft_tables(SEQ), DEC_SEQ: _dft_tables(DEC_SEQ)}
    return {k: tuple(jnp.asarray(m).astype(BF16) for m in v) for k, v in tabs.items()}


def _fourier(proj, tabs, t, nseq, row0, dst):
    full = lambda n: pl.BlockSpec((n, n), lambda b: (0, 0))
    in_specs = [_seq_spec(t, F_W, COL_D, row0), full(F_W), full(F_W), full(t), full(t)]
    return _seq_call(_fourier_kernel, "fourier", t, nseq, row0, in_specs,
                     [proj, *tabs["chan"], *tabs[t]], F_W, dst, [])


def _lam_of(lq1_ref, lk1_ref, lq2_ref, lk2_ref, lam_init):
    s1 = jnp.sum(lq1_ref[...] * lk1_ref[...], axis=-1, keepdims=True)
    s2 = jnp.sum(lq2_ref[...] * lk2_ref[...], axis=-1, keepdims=True)
    return jnp.exp(s1) - jnp.exp(s2) + lam_init


LOGIT_SCALE = HEAD_DIM ** -0.5 * math.log2(math.e)


def _exp2_and_rsum(t):
    e = jnp.exp2(t - jnp.max(t, axis=-1, keepdims=True))
    return e, 1.0 / jnp.sum(e, axis=-1, keepdims=True)


def _transposed_chunks(k):
    return [k[:, c * HEAD_DIM:(c + 1) * HEAD_DIM].T.astype(BF16) for c in range(ATTN_W // HEAD_DIM)]


def _diff_attn_heads(q, kt_of, v_of, lam, subln, lam_init, o_ref):
    def logits(h):
        maps = []
        for m in range(2):
            c = 2 * h + m
            qc = (q[:, c * HEAD_DIM:(c + 1) * HEAD_DIM] * LOGIT_SCALE).astype(BF16)
            maps.append(jnp.dot(qc, kt_of(c), preferred_element_type=F32))
        return maps

    pending = logits(0)
    for h in range(N_HEADS):
        following = logits(h + 1) if h + 1 < N_HEADS else None
        (e1, r1), (e2, r2) = (_exp2_and_rsum(t) for t in pending)
        a = (e1 * r1 - e2 * (lam * r2)).astype(BF16)
        o = jnp.dot(a, v_of(h), preferred_element_type=F32)
        o = _rms(o) * subln * (1.0 - lam_init)
        o_ref[:, h * V_DIM:(h + 1) * V_DIM] = o.astype(BF16)
        pending = following


def _attn_ctx_kernel(q_ref, k_ref, v_ref, lq1, lk1, lq2, lk2, sg_ref, *rest, lam_init):
    o_ref, nk_ref, nv_ref = rest[-3:]
    lam = _lam_of(lq1, lk1, lq2, lk2, lam_init)
    k = k_ref[...]
    v = v_ref[...]
    nk_ref[...] = k
    nv_ref[...] = v
    kt = _transposed_chunks(k)
    vb = v.astype(BF16)
    _diff_attn_heads(q_ref[...],
                     lambda c: kt[c],
                     lambda h: vb[:, h * V_DIM:(h + 1) * V_DIM],
                     lam, sg_ref[...], lam_init, o_ref)


def _attn_ctx(proj, lam_q1, lam_k1, lam_q2, lam_k2, subln_g, layer, new_kv):
    lam_init = 0.8 - 0.6 * math.exp(-0.3 * layer)
    lv = _layer_vec_spec(HEAD_DIM, layer)
    in_specs = [_seq_spec(SEQ, ATTN_W, COL_Q, 0), _seq_spec(SEQ, ATTN_W, COL_K, 0),
                _seq_spec(SEQ, ATTN_W, COL_V, 0), lv, lv, lv, lv, _layer_vec_spec(V_DIM, layer)]
    args = [proj, proj, proj, lam_q1, lam_k1, lam_q2, lam_k2, subln_g]
    aliases = {}
    if new_kv is not None:
        in_specs += [pl.BlockSpec(memory_space=pl.ANY)] * 2
        aliases = {len(args): 1, len(args) + 1: 2}
        args += list(new_kv)
    kv_spec = pl.BlockSpec((None, None, SEQ, ATTN_W), lambda b: (b, layer, 0, 0))
    kv_shape = jax.ShapeDtypeStruct((BATCH, DEPTH, SEQ, ATTN_W), F32)
    yb, nk, nv = pl.pallas_call(
        functools.partial(_attn_ctx_kernel, lam_init=lam_init),
        grid=(BATCH,),
        in_specs=in_specs,
        out_specs=[pl.BlockSpec((SEQ, ATTN_W), lambda b: (b, 0)), kv_spec, kv_spec],
        out_shape=[jax.ShapeDtypeStruct((N_ROWS, ATTN_W), BF16), kv_shape, kv_shape],
        input_output_aliases=aliases,
        compiler_params=_params(1),
        name="attn_ctx",
    )(*args)
    return yb, (nk, nv)


def _rope(x, cos, sin_a, sin_b):
    outs = []
    for c in range(ATTN_W // HEAD_DIM):
        xc = x[:, c * HEAD_DIM:(c + 1) * HEAD_DIM]
        fwd = pltpu.roll(xc, HEAD_DIM - HEAD_DIM // 4, 1)
        bwd = pltpu.roll(xc, HEAD_DIM // 4, 1)
        outs.append(xc * cos + fwd * sin_a + bwd * sin_b)
    return outs


def _attn_lat_kernel(q_ref, k_ref, v_ref, ck_ref, cv_ref, cosq_ref, sinaq_ref, sinbq_ref,
                     cos_ref, sina_ref, sinb_ref, lq1, lk1, lq2, lk2, sg_ref, dst_ref, o_ref,
                     kt_ref, vall_ref, *, lam_init):
    @pl.when(pl.program_id(1) == 0)
    def _():
        vall_ref[0:PAST_LEN, :] = cv_ref[...].astype(BF16)
        vall_ref[PAST_LEN:PAST_LEN + DEC_SEQ, :] = v_ref[...].astype(BF16)
        for c, kc in enumerate(_transposed_chunks(ck_ref[...])):
            kt_ref[c * HEAD_DIM:(c + 1) * HEAD_DIM, 0:PAST_LEN] = kc
        for c in range(ATTN_W // HEAD_DIM):
            kt = k_ref[:, c * HEAD_DIM:(c + 1) * HEAD_DIM].T
            fwd = pltpu.roll(kt, HEAD_DIM - HEAD_DIM // 4, 0)
            bwd = pltpu.roll(kt, HEAD_DIM // 4, 0)
            kr = kt * cos_ref[...] + fwd * sina_ref[...] + bwd * sinb_ref[...]
            kt_ref[c * HEAD_DIM:(c + 1) * HEAD_DIM, PAST_LEN:PAST_LEN + DEC_SEQ] = kr.astype(BF16)

    lam = _lam_of(lq1, lk1, lq2, lk2, lam_init)
    q = jnp.concatenate(_rope(q_ref[...], cosq_ref[...], sinaq_ref[...], sinbq_ref[...]), axis=-1)
    _diff_attn_heads(q,
                     lambda c: kt_ref[c * HEAD_DIM:(c + 1) * HEAD_DIM, :],
                     lambda h: vall_ref[:, h * V_DIM:(h + 1) * V_DIM],
                     lam, sg_ref[...], lam_init, o_ref)


def _rope_tables():
    t = jnp.arange(DEC_SEQ)
    row = (t // GRID_W).astype(F32)
    col = (t % GRID_W).astype(F32)
    half = HEAD_DIM // 2
    inv = ROPE_THETA ** (-jnp.arange(0, half, 2, dtype=F32) / half)
    ar = row[:, None] * inv
    ac = col[:, None] * inv
    ang = jnp.concatenate([ar, ar, ac, ac], axis=-1)
    cos, sin = jnp.cos(ang), jnp.sin(ang)
    first = (np.arange(HEAD_DIM) % half) < (half // 2)
    sin_a = jnp.where(first, -sin, 0.0)
    sin_b = jnp.where(first, 0.0, sin)
    return cos, sin_a, sin_b, cos.T, sin_a.T, sin_b.T


def _attn_lat(proj, cache_k, cache_v, rope, lam_q1, lam_k1, lam_q2, lam_k2, subln_g, layer, dst):
    lam_init = 0.8 - 0.6 * math.exp(-0.3 * layer)
    tq = 256
    nq = DEC_SEQ // tq
    lv = _layer_vec_spec(HEAD_DIM, layer)
    qtab = pl.BlockSpec((tq, HEAD_DIM), lambda b, i: (i, 0))
    ktab = pl.BlockSpec((HEAD_DIM, DEC_SEQ), lambda b, i: (0, 0))
    cache = pl.BlockSpec((None, None, PAST_LEN, ATTN_W), lambda b, i: (b, layer, 0, 0))

    def kv(col):
        return pl.BlockSpec((DEC_SEQ, ATTN_W), lambda b, i: (P_ROWS // DEC_SEQ + b, col // ATTN_W))

    args = [proj, proj, proj, cache_k, cache_v, *rope, lam_q1, lam_k1, lam_q2, lam_k2, subln_g, dst]
    return pl.pallas_call(
        functools.partial(_attn_lat_kernel, lam_init=lam_init),
        grid=(DEC_BATCH, nq),
        in_specs=[pl.BlockSpec((tq, ATTN_W), lambda b, i: (P_ROWS // tq + b * nq + i, COL_Q // ATTN_W)),
                  kv(COL_K), kv(COL_V), cache, cache, qtab, qtab, qtab, ktab, ktab, ktab,
                  lv, lv, lv, lv, _layer_vec_spec(V_DIM, layer), pl.BlockSpec(memory_space=pl.ANY)],
        out_specs=pl.BlockSpec((tq, ATTN_W), lambda b, i: (P_ROWS // tq + b * nq + i, 0)),
        out_shape=jax.ShapeDtypeStruct((N_ROWS, ATTN_W), BF16),
        scratch_shapes=[pltpu.VMEM((ATTN_W, PAST_LEN + DEC_SEQ), BF16),
                        pltpu.VMEM((PAST_LEN + DEC_SEQ, ATTN_W), BF16)],
        input_output_aliases={len(args) - 1: 0},
        compiler_params=_params(2),
        name="attn_lat",
    )(*args)


def kernel(x_prompt, x_sample, cache_k, cache_v, c, c_ctx, w_ada, b_ada, g_pre_mix, g_post_mix,
           g_pre_ffn, g_post_ffn, w_in, conv_a_w, conv_a_b, ln_a_g, ln_a_b, w_a_out,
           lam_q1, lam_k1, lam_q2, lam_k2, subln_g, w_b_out, conv_c_w, w_c_out, w_d_out,
           w_gate, b_gate, w_o, w_ffn_up, w_ffn_down):
    x = jnp.concatenate([x_prompt.reshape(P_ROWS, D_MODEL), x_sample.reshape(S_ROWS, D_MODEL)], axis=0)
    cs = jnp.concatenate([c_ctx[None, :], c, jnp.zeros((ADA_ROWS - N_GROUPS, D_MODEL), F32)], axis=0)
    mod = _ada(cs, w_ada, b_ada)[:, :N_GROUPS].reshape(DEPTH * N_GROUPS * N_MOD, 1, D_MODEL)

    vec = lambda a: a.reshape(DEPTH, 1, a.shape[-1])
    g_pre_mix, g_post_mix, g_pre_ffn, g_post_ffn = map(vec, (g_pre_mix, g_post_mix, g_pre_ffn, g_post_ffn))
    conv_a_b, ln_a_g, ln_a_b, subln_g, b_gate = map(vec, (conv_a_b, ln_a_g, ln_a_b, subln_g, b_gate))
    lam_q1, lam_k1, lam_q2, lam_k2 = map(vec, (lam_q1, lam_k1, lam_q2, lam_k2))
    cache_k = cache_k.reshape(DEC_BATCH, DEPTH, PAST_LEN, ATTN_W)
    cache_v = cache_v.reshape(DEC_BATCH, DEPTH, PAST_LEN, ATTN_W)
    rope = _rope_tables()
    dft = _fourier_tables()

    def both(fn, *args):
        return fn(*args, DEC_SEQ, DEC_BATCH, P_ROWS, fn(*args, SEQ, BATCH, 0, None))

    h = _normmod(x, g_pre_mix, mod, 0, 1, 0)
    new_kv = None
    for l in range(DEPTH):
        proj = _mm(h, w_in, l, tm=1024, tn=1024, out_dtype=F32, name="mm_in")
        gates = _mm(h, w_gate, l, tm=1024, tn=1024, out_dtype=BF16, bias=b_gate, name="mm_gate")

        ya = both(_conv_a, proj, conv_a_w, conv_a_b, ln_a_g, ln_a_b, l)
        yc = both(_conv_c, proj, conv_c_w, l)
        yd = both(_fourier, proj, dft)
        lam_args = (lam_q1, lam_k1, lam_q2, lam_k2, subln_g, l)
        yb, new_kv = _attn_ctx(proj, *lam_args, new_kv)
        yb = _attn_lat(proj, cache_k, cache_v, rope, *lam_args, yb)

        merged = _merge(ya, yb, yc, yd, gates, w_a_out, w_b_out, w_c_out, w_d_out, l)
        x, h2 = _mm_resid(merged, w_o, x, mod, l, 2, g_post_mix, (g_pre_ffn, l, 4, 3),
                          tm=512, piece=128, chunk=256, name="mm_o")
        act = _ffn_up(h2, w_ffn_up, l)
        nxt = (g_pre_mix, l + 1, 1, 0) if l + 1 < DEPTH else None
        res = _mm_resid(act, w_ffn_down, x, mod, l, 5, g_post_ffn, nxt,
                        tm=256, piece=128, chunk=256, name="mm_down")
        x, h = res if nxt is not None else (res[0], None)

    y_prompt = x[:P_ROWS].reshape(BATCH, SEQ, D_MODEL)
    y_sample = x[P_ROWS:].reshape(DEC_BATCH, DEC_SEQ, D_MODEL)
    new_k, new_v = (a.reshape(BATCH, DEPTH, SEQ, N_HEADS, V_DIM) for a in new_kv)
    return y_prompt, y_sample, new_k, new_v
```

```python
import functools
import math

import numpy as np
import jax
import jax.numpy as jnp
from jax import lax
from jax.experimental import pallas as pl
from jax.experimental.pallas import tpu as pltpu

D_MODEL = 2048
BATCH = 16
SEQ = 256
DEPTH = 4
DEC_BATCH = 2
DEC_SEQ = 1024
PAST_LEN = 512
GRID_W = 64
N_HEADS = 4
HEAD_DIM = 128
V_DIM = 2 * HEAD_DIM
ATTN_W = N_HEADS * 2 * HEAD_DIM
CONV_W = 512
CONV_K = 31
SC_W = 512
SC_K = 3
F_W = 512
F_GROUPS = 4
F_GROUP = F_W // F_GROUPS
N_BRANCH = 4
D_IN = 2 * CONV_W + 3 * ATTN_W + 3 * SC_W + F_W
D_FF = ((8 * D_MODEL + 3 * 256 - 1) // (3 * 256)) * 256
ROPE_THETA = 10000.0
EPS = 1e-6

P_ROWS = BATCH * SEQ
S_ROWS = DEC_BATCH * DEC_SEQ
N_ROWS = P_ROWS + S_ROWS
N_GROUPS = 1 + DEC_BATCH
N_MOD = 6
ADA_ROWS = 8

COL_A = 0
COL_Q = 2 * CONV_W
COL_K = COL_Q + ATTN_W
COL_V = COL_K + ATTN_W
COL_SB = COL_V + ATTN_W
COL_SC = COL_SB + SC_W
COL_SX = COL_SC + SC_W
COL_D = COL_SX + SC_W

SUBLANES = 8
CONV_PAD = 16
CONV_ROWS = 64
VMEM_LIMIT = 56 * 1024 * 1024
MM_PIECE = 256

F32 = jnp.float32
BF16 = jnp.bfloat16


def _params(n_axes, vmem=VMEM_LIMIT):
    return pltpu.CompilerParams(dimension_semantics=("arbitrary",) * n_axes, vmem_limit_bytes=vmem)


def _group_of_tile(i, tm):
    pt = P_ROWS // tm
    st = DEC_SEQ // tm
    return jnp.maximum(i - (pt - st), 0) // st


def _mod_spec(layer, which, tm, tile_of):
    def index(*ids):
        g = _group_of_tile(tile_of(*ids), tm)
        return (layer * N_GROUPS * N_MOD + g * N_MOD + which, 0, 0)
    return pl.BlockSpec((None, 1, D_MODEL), index)


def _layer_vec_spec(width, layer):
    return pl.BlockSpec((None, 1, width), lambda *ids: (layer, 0, 0))


def _ada_kernel(c_ref, w_ref, b_ref, o_ref):
    s = _silu(c_ref[...]).astype(BF16)
    o_ref[...] = jnp.dot(s, w_ref[...].astype(BF16), preferred_element_type=F32) + b_ref[...]


def _ada(cs, w_ada, b_ada):
    tn = 1024
    n = N_MOD * D_MODEL
    return pl.pallas_call(
        _ada_kernel,
        grid=(DEPTH, n // tn),
        in_specs=[
            pl.BlockSpec((ADA_ROWS, D_MODEL), lambda l, j: (0, 0)),
            pl.BlockSpec((None, D_MODEL, tn), lambda l, j: (l, 0, j)),
            pl.BlockSpec((None, 1, tn), lambda l, j: (l, 0, j)),
        ],
        out_specs=pl.BlockSpec((None, ADA_ROWS, tn), lambda l, j: (l, 0, j)),
        out_shape=jax.ShapeDtypeStruct((DEPTH, ADA_ROWS, n), F32),
        compiler_params=_params(2),
        name="ada",
    )(cs, w_ada, b_ada.reshape(DEPTH, 1, n))


def _rms(x):
    return x * lax.rsqrt(jnp.mean(x * x, axis=-1, keepdims=True) + EPS)


def _sigmoid(x):
    return 0.5 * jnp.tanh(0.5 * x) + 0.5


def _silu(x):
    return x * _sigmoid(x)


def _normmod_kernel(x_ref, g_ref, sc_ref, sh_ref, h_ref):
    y = _rms(x_ref[...]) * g_ref[...]
    h_ref[...] = (y * (1.0 + sc_ref[...]) + sh_ref[...]).astype(BF16)


def _normmod(x, g, mod, layer, which_sc, which_sh):
    tm = 256
    row = pl.BlockSpec((tm, D_MODEL), lambda i: (i, 0))
    tile = lambda i: i
    return pl.pallas_call(
        _normmod_kernel,
        grid=(N_ROWS // tm,),
        in_specs=[row, _layer_vec_spec(D_MODEL, layer),
                  _mod_spec(layer, which_sc, tm, tile), _mod_spec(layer, which_sh, tm, tile)],
        out_specs=row,
        out_shape=jax.ShapeDtypeStruct((N_ROWS, D_MODEL), BF16),
        compiler_params=_params(1),
        name="normmod",
    )(x, g, mod, mod)


def _cast_weight(w_ref, wbf_ref):
    rows = w_ref.shape[0]
    chunk = 256

    def body(r, carry):
        r0 = pl.multiple_of(r * chunk, chunk)
        wbf_ref[pl.ds(r0, chunk), :] = w_ref[pl.ds(r0, chunk), :].astype(BF16)
        return carry

    lax.fori_loop(0, rows // chunk, body, 0)


def _mm_kernel(x_ref, w_ref, *rest, sigmoid_bias):
    if sigmoid_bias:
        b_ref, o_ref, wbf_ref = rest
    else:
        o_ref, wbf_ref = rest

    @pl.when(pl.program_id(1) == 0)
    def _():
        _cast_weight(w_ref, wbf_ref)

    x = x_ref[...]
    for c in range(0, o_ref.shape[1], MM_PIECE):
        cols = slice(c, c + MM_PIECE)
        acc = jnp.dot(x, wbf_ref[:, cols], preferred_element_type=F32)
        if sigmoid_bias:
            acc = _sigmoid(acc + b_ref[:, cols])
        o_ref[:, cols] = acc.astype(o_ref.dtype)


def _mm(x, w, layer, *, tm, tn, out_dtype, bias=None, name):
    m, k = x.shape
    n = w.shape[2]
    in_specs = [pl.BlockSpec((tm, k), lambda j, i: (i, 0)),
                pl.BlockSpec((None, k, tn), lambda j, i: (layer, 0, j))]
    args = [x, w]
    if bias is not None:
        in_specs.append(pl.BlockSpec((None, 1, tn), lambda j, i: (layer, 0, j)))
        args.append(bias)
    return pl.pallas_call(
        functools.partial(_mm_kernel, sigmoid_bias=bias is not None),
        grid=(n // tn, m // tm),
        in_specs=in_specs,
        out_specs=pl.BlockSpec((tm, tn), lambda j, i: (i, j)),
        out_shape=jax.ShapeDtypeStruct((m, n), out_dtype),
        scratch_shapes=[pltpu.VMEM((k, tn), BF16)],
        compiler_params=_params(2),
        name=name,
    )(*args)


def _merge_kernel(*refs, ctx_tiles):
    ctx_branches, lat_branches = refs[0:N_BRANCH], refs[N_BRANCH:2 * N_BRANCH]
    gate_refs = refs[2 * N_BRANCH:3 * N_BRANCH]
    w_refs = refs[3 * N_BRANCH:4 * N_BRANCH]
    o_ref = refs[4 * N_BRANCH]
    w_scratch = refs[4 * N_BRANCH + 1:]

    @pl.when(pl.program_id(1) == 0)
    def _():
        for w_ref, w_s in zip(w_refs, w_scratch):
            _cast_weight(w_ref, w_s)

    def merge(branches):
        acc = None
        for y_ref, g_ref, w_s in zip(branches, gate_refs, w_scratch):
            term = g_ref[...].astype(F32) * jnp.dot(y_ref[...], w_s[...], preferred_element_type=F32)
            acc = term if acc is None else acc + term
        o_ref[...] = acc.astype(BF16)

    is_ctx = pl.program_id(1) < ctx_tiles
    pl.when(is_ctx)(lambda: merge(ctx_branches))
    pl.when(jnp.logical_not(is_ctx))(lambda: merge(lat_branches))


def _merge(branches, gates, weights, layer):
    tm, tn = 512, 1024
    nt = D_MODEL // tn
    ctx_tiles = P_ROWS // tm
    widths = [w.shape[1] for w in weights]

    def ctx(width):
        return pl.BlockSpec((tm, width), lambda j, i: (jnp.minimum(i, ctx_tiles - 1), 0))

    def lat(width):
        return pl.BlockSpec((tm, width), lambda j, i: (jnp.maximum(i - ctx_tiles, 0), 0))

    def gate(branch):
        return pl.BlockSpec((tm, tn), lambda j, i: (i, branch * nt + j))

    def wt(width):
        return pl.BlockSpec((None, width, tn), lambda j, i: (layer, 0, j))

    return pl.pallas_call(
        functools.partial(_merge_kernel, ctx_tiles=ctx_tiles),
        grid=(nt, N_ROWS // tm),
        in_specs=([ctx(w) for w in widths] + [lat(w) for w in widths]
                  + [gate(b) for b in range(N_BRANCH)] + [wt(w) for w in widths]),
        out_specs=pl.BlockSpec((tm, tn), lambda j, i: (i, j)),
        out_shape=jax.ShapeDtypeStruct((N_ROWS, D_MODEL), BF16),
        scratch_shapes=[pltpu.VMEM((w, tn), BF16) for w in widths],
        compiler_params=_params(2),
        name="merge",
    )(*[b[0] for b in branches], *[b[1] for b in branches], *([gates] * N_BRANCH), *weights)


def _ffn_up_kernel(x_ref, wu_ref, wg_ref, o_ref, wu_s, wg_s):
    @pl.when(pl.program_id(1) == 0)
    def _():
        _cast_weight(wu_ref, wu_s)
        _cast_weight(wg_ref, wg_s)

    x = x_ref[...]
    for c in range(0, o_ref.shape[1], MM_PIECE):
        cols = slice(c, c + MM_PIECE)
        up = jnp.dot(x, wu_s[:, cols], preferred_element_type=F32)
        gt = jnp.dot(x, wg_s[:, cols], preferred_element_type=F32)
        o_ref[:, cols] = (_silu(gt) * up).astype(BF16)


def _ffn_up(h, w_up, layer):
    tm, tn = 2048, 512
    nt = D_FF // tn
    return pl.pallas_call(
        _ffn_up_kernel,
        grid=(nt, N_ROWS // tm),
        in_specs=[pl.BlockSpec((tm, D_MODEL), lambda j, i: (i, 0)),
                  pl.BlockSpec((None, D_MODEL, tn), lambda j, i: (layer, 0, j)),
                  pl.BlockSpec((None, D_MODEL, tn), lambda j, i: (layer, 0, nt + j))],
        out_specs=pl.BlockSpec((tm, tn), lambda j, i: (i, j)),
        out_shape=jax.ShapeDtypeStruct((N_ROWS, D_FF), BF16),
        scratch_shapes=[pltpu.VMEM((D_MODEL, tn), BF16), pltpu.VMEM((D_MODEL, tn), BF16)],
        compiler_params=_params(2),
        name="ffn_up",
    )(h, w_up, w_up)


EPI_ROWS = 16


def _mm_resid_kernel(*refs, nw, chunk, tm, piece, with_next):
    if with_next:
        a_ref, w_ref, x_ref, gate_ref, gpost_ref, gpre_ref, sc_ref, sh_ref, xo_ref, h_ref, wbf_ref = refs
    else:
        a_ref, w_ref, x_ref, gate_ref, gpost_ref, yctx_ref, ylat_ref, wbf_ref = refs
    s = pl.program_id(0)

    @pl.when(s < nw)
    def _():
        r0 = pl.multiple_of(s * chunk, chunk)
        wbf_ref[pl.ds(r0, chunk), :] = w_ref[...].astype(BF16)

    def row_tile(xo_ref):
        gg = gate_ref[...] * gpost_ref[...]
        if with_next:
            pp = gpre_ref[...] * (1.0 + sc_ref[...])
            sh = sh_ref[...]
        for p0 in range(0, tm, piece):
            f = jnp.dot(a_ref[p0:p0 + piece, :], wbf_ref[...], preferred_element_type=F32)
            for r in range(0, piece, EPI_ROWS):
                rows = slice(p0 + r, p0 + r + EPI_ROWS)
                xn = x_ref[rows, :] + _rms(f[r:r + EPI_ROWS, :]) * gg
                xo_ref[rows, :] = xn
                if with_next:
                    h_ref[rows, :] = (_rms(xn) * pp + sh).astype(BF16)

    if with_next:
        pl.when(s >= nw)(lambda: row_tile(xo_ref))
    else:
        first_lat = nw + P_ROWS // tm
        pl.when((s >= nw) & (s < first_lat))(lambda: row_tile(yctx_ref))
        pl.when(s >= first_lat)(lambda: row_tile(ylat_ref))


def _mm_resid(a, w, x, mod, layer, which_gate, g_post, nxt, *, tm, piece, chunk, name):
    k = a.shape[1]
    nw = k // chunk
    nt = N_ROWS // tm
    ctx_tiles = P_ROWS // tm
    tile = lambda s: jnp.maximum(s - nw, 0)
    row = pl.BlockSpec((tm, D_MODEL), lambda s: (tile(s), 0))
    in_specs = [pl.BlockSpec((tm, k), lambda s: (tile(s), 0)),
                pl.BlockSpec((None, chunk, D_MODEL), lambda s: (layer, jnp.minimum(s, nw - 1), 0)),
                row, _mod_spec(layer, which_gate, tm, tile), _layer_vec_spec(D_MODEL, layer)]
    args = [a, w, x, mod, g_post]
    if nxt is not None:
        g_pre, nl, wsc, wsh = nxt
        in_specs += [_layer_vec_spec(D_MODEL, nl), _mod_spec(nl, wsc, tm, tile), _mod_spec(nl, wsh, tm, tile)]
        args += [g_pre, mod, mod]
        out_specs = [row, row]
        out_shape = [jax.ShapeDtypeStruct((N_ROWS, D_MODEL), F32), jax.ShapeDtypeStruct((N_ROWS, D_MODEL), BF16)]
    else:
        out_specs = [pl.BlockSpec((tm, D_MODEL), lambda s: (jnp.minimum(tile(s), ctx_tiles - 1), 0)),
                     pl.BlockSpec((tm, D_MODEL), lambda s: (jnp.maximum(tile(s) - ctx_tiles, 0), 0))]
        out_shape = [jax.ShapeDtypeStruct((P_ROWS, D_MODEL), F32), jax.ShapeDtypeStruct((S_ROWS, D_MODEL), F32)]
    return pl.pallas_call(
        functools.partial(_mm_resid_kernel, nw=nw, chunk=chunk, tm=tm, piece=piece, with_next=nxt is not None),
        grid=(nw + nt,),
        in_specs=in_specs,
        out_specs=out_specs,
        out_shape=out_shape,
        scratch_shapes=[pltpu.VMEM((k, D_MODEL), BF16)],
        compiler_params=_params(1),
        name=name,
    )(*args)


def _seq_spec(t, width, col, row0):
    return pl.BlockSpec((t, width), lambda b: (row0 // t + b, col // width))


def _seq_call(kernel_fn, name, t, nseq, in_specs, args, width, scratch):
    return pl.pallas_call(
        kernel_fn,
        grid=(nseq,),
        in_specs=in_specs,
        out_specs=pl.BlockSpec((t, width), lambda b: (b, 0)),
        out_shape=jax.ShapeDtypeStruct((nseq * t, width), BF16),
        scratch_shapes=scratch,
        compiler_params=_params(1),
        name=name,
    )(*args)


def _dwconv_chunk(pad_ref, w_ref, r0, ksize):
    win = CONV_ROWS + 2 * CONV_PAD
    window = pad_ref[r0:r0 + win, :]
    acc = None
    for b in range(SUBLANES):
        taps = [j for j in range(ksize) if (CONV_PAD - ksize // 2 + j) % SUBLANES == b]
        if not taps:
            continue
        shifted = window if b == 0 else pltpu.roll(window, win - b, 0)
        for j in taps:
            a0 = (CONV_PAD - ksize // 2 + j) - b
            term = w_ref[j:j + 1, :] * shifted[a0:a0 + CONV_ROWS, :]
            acc = term if acc is None else acc + term
    return acc


def _fill_pad(pad_ref, u, t):
    zeros = jnp.zeros((CONV_PAD, pad_ref.shape[1]), F32)
    pad_ref[0:CONV_PAD, :] = zeros
    pad_ref[CONV_PAD + t:2 * CONV_PAD + t, :] = zeros
    pad_ref[CONV_PAD:CONV_PAD + t, :] = u


def _conv_a_kernel(a_ref, w_ref, cb_ref, lg_ref, lb_ref, o_ref, pad_ref, *, t):
    _fill_pad(pad_ref, a_ref[:, 0:CONV_W] * _sigmoid(a_ref[:, CONV_W:2 * CONV_W]), t)
    for r0 in range(0, t, CONV_ROWS):
        acc = _dwconv_chunk(pad_ref, w_ref, r0, CONV_K) + cb_ref[...]
        mu = jnp.mean(acc, axis=-1, keepdims=True)
        xc = acc - mu
        y = xc * lax.rsqrt(jnp.mean(xc * xc, axis=-1, keepdims=True) + EPS)
        y = y * lg_ref[...] + lb_ref[...]
        o_ref[r0:r0 + CONV_ROWS, :] = _silu(y).astype(BF16)


def _conv_a(proj, conv_w, conv_b, ln_g, ln_b, layer, t, nseq, row0):
    in_specs = [_seq_spec(t, 2 * CONV_W, COL_A, row0),
                pl.BlockSpec((None, CONV_K, CONV_W), lambda b: (layer, 0, 0)),
                _layer_vec_spec(CONV_W, layer), _layer_vec_spec(CONV_W, layer),
                _layer_vec_spec(CONV_W, layer)]
    return _seq_call(functools.partial(_conv_a_kernel, t=t), "conv_a", t, nseq, in_specs,
                     [proj, conv_w, conv_b, ln_g, ln_b], CONV_W,
                     [pltpu.VMEM((t + 2 * CONV_PAD, CONV_W), F32)])


def _conv_c_kernel(sb_ref, sc_ref, sx_ref, w_ref, o_ref, pad_ref, *, t):
    _fill_pad(pad_ref, sc_ref[...] * sx_ref[...], t)
    for r0 in range(0, t, CONV_ROWS):
        acc = _dwconv_chunk(pad_ref, w_ref, r0, SC_K)
        o_ref[r0:r0 + CONV_ROWS, :] = (sb_ref[r0:r0 + CONV_ROWS, :] * acc).astype(BF16)


def _conv_c(proj, conv_w, layer, t, nseq, row0):
    in_specs = [_seq_spec(t, SC_W, COL_SB, row0), _seq_spec(t, SC_W, COL_SC, row0),
                _seq_spec(t, SC_W, COL_SX, row0),
                pl.BlockSpec((None, SC_K, SC_W), lambda b: (layer, 0, 0))]
    return _seq_call(functools.partial(_conv_c_kernel, t=t), "conv_c", t, nseq, in_specs,
                     [proj, proj, proj, conv_w], SC_W,
                     [pltpu.VMEM((t + 2 * CONV_PAD, SC_W), F32)])


def _fourier_kernel(u_ref, cc_ref, sc_ref, ct_ref, st_ref, o_ref):
    u =u_ref[...].astype(BF16)
    yc = jnp.dot(u, cc_ref[...], preferred_element_type=F32).astype(BF16)
    ys = jnp.dot(u, sc_ref[...], preferred_element_type=F32).astype(BF16)
    f = (jnp.dot(ct_ref[...], yc, preferred_element_type=F32)
         - jnp.dot(st_ref[...], ys, preferred_element_type=F32))
    o_ref[...] = f.astype(BF16)


def _dft_tables(n):
    idx = np.arange(n)
    ang = 2.0 * np.pi * ((idx[:, None] * idx[None, :]) % n) / n
    s = 1.0 / math.sqrt(n)
    return (np.cos(ang) * s).astype(np.float32), (np.sin(ang) * s).astype(np.float32)


def _fourier_tables():
    cg, sg = _dft_tables(F_GROUP)
    eye = np.eye(F_GROUPS, dtype=np.float32)
    tabs = {"chan": (np.kron(eye, cg), np.kron(eye, sg)), SEQ: _dft_tables(SEQ), DEC_SEQ: _dft_tables(DEC_SEQ)}
    return {k: tuple(jnp.asarray(m).astype(BF16) for m in v) for k, v in tabs.items()}


def _fourier(proj, tabs, t, nseq, row0):
    full = lambda n: pl.BlockSpec((n, n), lambda b: (0, 0))
    in_specs = [_seq_spec(t, F_W, COL_D, row0), full(F_W), full(F_W), full(t), full(t)]
    return _seq_call(_fourier_kernel, "fourier", t, nseq, in_specs,
                     [proj, *tabs["chan"], *tabs[t]], F_W, [])


def _lam_of(lq1_ref, lk1_ref, lq2_ref, lk2_ref, lam_init):
    s1 = jnp.sum(lq1_ref[...] * lk1_ref[...], axis=-1, keepdims=True)
    s2 = jnp.sum(lq2_ref[...] * lk2_ref[...], axis=-1, keepdims=True)
    return jnp.exp(s1) - jnp.exp(s2) + lam_init


LOGIT_SCALE = HEAD_DIM ** -0.5 * math.log2(math.e)


def _exp2_and_rsum(t):
    e = jnp.exp2(t - jnp.max(t, axis=-1, keepdims=True))
    return e, 1.0 / jnp.sum(e, axis=-1, keepdims=True)


def _transposed_chunks(k):
    return [k[:, c * HEAD_DIM:(c + 1) * HEAD_DIM].T.astype(BF16) for c in range(ATTN_W // HEAD_DIM)]


def _diff_attn_heads(q, kt_of, v_of, lam, subln, lam_init, o_ref):
    def logits(h):
        maps = []
        for m in range(2):
            c = 2 * h + m
            qc = (q[:, c * HEAD_DIM:(c + 1) * HEAD_DIM] * LOGIT_SCALE).astype(BF16)
            maps.append(jnp.dot(qc, kt_of(c), preferred_element_type=F32))
        return maps

    pending = logits(0)
    for h in range(N_HEADS):
        following = logits(h + 1) if h + 1 < N_HEADS else None
        (e1, r1), (e2, r2) = (_exp2_and_rsum(t) for t in pending)
        a = (e1 * r1 - e2 * (lam * r2)).astype(BF16)
        o = jnp.dot(a, v_of(h), preferred_element_type=F32)
        o = _rms(o) * subln * (1.0 - lam_init)
        o_ref[:, h * V_DIM:(h + 1) * V_DIM] = o.astype(BF16)
        pending = following


def _attn_ctx_kernel(q_ref, k_ref, v_ref, lq1, lk1, lq2, lk2, sg_ref, o_ref, nk_ref, nv_ref, *, lam_init):
    lam =_lam_of(lq1, lk1, lq2, lk2, lam_init)
    k = k_ref[...]
    v = v_ref[...]
    nk_ref[...] = k
    nv_ref[...] = v
    kt = _transposed_chunks(k)
    vb = v.astype(BF16)
    _diff_attn_heads(q_ref[...],
                     lambda c: kt[c],
                     lambda h: vb[:, h * V_DIM:(h + 1) * V_DIM],
                     lam, sg_ref[...], lam_init, o_ref)


def _attn_ctx(proj, lam_q1, lam_k1, lam_q2, lam_k2, subln_g, layer):
    lam_init = 0.8 - 0.6 * math.exp(-0.3 * layer)
    lv = _layer_vec_spec(HEAD_DIM, layer)
    in_specs = [_seq_spec(SEQ, ATTN_W, COL_Q, 0), _seq_spec(SEQ, ATTN_W, COL_K, 0),
                _seq_spec(SEQ, ATTN_W, COL_V, 0), lv, lv, lv, lv, _layer_vec_spec(V_DIM, layer)]
    args = [proj, proj, proj, lam_q1, lam_k1, lam_q2, lam_k2, subln_g]
    seq = pl.BlockSpec((SEQ, ATTN_W), lambda b: (b, 0))
    kv_shape = jax.ShapeDtypeStruct((P_ROWS, ATTN_W), F32)
    return pl.pallas_call(
        functools.partial(_attn_ctx_kernel, lam_init=lam_init),
        grid=(BATCH,),
        in_specs=in_specs,
        out_specs=[seq, seq, seq],
        out_shape=[jax.ShapeDtypeStruct((P_ROWS, ATTN_W), BF16), kv_shape, kv_shape],
        compiler_params=_params(1),
        name="attn_ctx",
    )(*args)


def _rope(x, cos, sin_a, sin_b):
    outs = []
    for c in range(ATTN_W // HEAD_DIM):
        xc = x[:, c * HEAD_DIM:(c + 1) * HEAD_DIM]
        fwd = pltpu.roll(xc, HEAD_DIM - HEAD_DIM // 4, 1)
        bwd = pltpu.roll(xc, HEAD_DIM // 4, 1)
        outs.append(xc * cos + fwd * sin_a + bwd * sin_b)
    return outs


def _attn_lat_kernel(q_ref, k_ref, v_ref, ck_ref, cv_ref, cosq_ref, sinaq_ref, sinbq_ref,
                     cos_ref, sina_ref, sinb_ref, lq1, lk1, lq2, lk2, sg_ref, o_ref,
                     kt_ref, vall_ref, *, lam_init):
    @pl.when(pl.program_id(1) == 0)
    def _():
        vall_ref[0:PAST_LEN, :] = cv_ref[...].astype(BF16)
        vall_ref[PAST_LEN:PAST_LEN + DEC_SEQ, :] = v_ref[...].astype(BF16)
        for c, kc in enumerate(_transposed_chunks(ck_ref[...])):
            kt_ref[c * HEAD_DIM:(c + 1) * HEAD_DIM, 0:PAST_LEN] = kc
        for c in range(ATTN_W // HEAD_DIM):
            kt = k_ref[:, c * HEAD_DIM:(c + 1) * HEAD_DIM].T
            fwd = pltpu.roll(kt, HEAD_DIM - HEAD_DIM // 4, 0)
            bwd = pltpu.roll(kt, HEAD_DIM // 4, 0)
            kr = kt * cos_ref[...] + fwd * sina_ref[...] + bwd * sinb_ref[...]
            kt_ref[c * HEAD_DIM:(c + 1) * HEAD_DIM, PAST_LEN:PAST_LEN + DEC_SEQ] = kr.astype(BF16)

    lam = _lam_of(lq1, lk1, lq2, lk2, lam_init)
    q = jnp.concatenate(_rope(q_ref[...], cosq_ref[...], sinaq_ref[...], sinbq_ref[...]), axis=-1)
    _diff_attn_heads(q,
                     lambda c: kt_ref[c * HEAD_DIM:(c + 1) * HEAD_DIM, :],
                     lambda h: vall_ref[:, h * V_DIM:(h + 1) * V_DIM],
                     lam, sg_ref[...], lam_init, o_ref)


def _rope_tables():
    t = jnp.arange(DEC_SEQ)
    row = (t // GRID_W).astype(F32)
    col = (t % GRID_W).astype(F32)
    half = HEAD_DIM // 2
    inv = ROPE_THETA ** (-jnp.arange(0, half, 2, dtype=F32) / half)
    ar = row[:, None] * inv
    ac = col[:, None] * inv
    ang = jnp.concatenate([ar, ar, ac, ac], axis=-1)
    cos, sin = jnp.cos(ang), jnp.sin(ang)
    first = (np.arange(HEAD_DIM) % half) < (half // 2)
    sin_a = jnp.where(first, -sin, 0.0)
    sin_b = jnp.where(first, 0.0, sin)
    return cos, sin_a, sin_b, cos.T, sin_a.T, sin_b.T


def _attn_lat(proj, cache_k, cache_v, rope, lam_q1, lam_k1, lam_q2, lam_k2, subln_g, layer):
    lam_init = 0.8 - 0.6 * math.exp(-0.3 * layer)
    tq = 256
    nq = DEC_SEQ // tq
    lv = _layer_vec_spec(HEAD_DIM, layer)
    qtab = pl.BlockSpec((tq, HEAD_DIM), lambda b, i: (i, 0))
    ktab = pl.BlockSpec((HEAD_DIM, DEC_SEQ), lambda b, i: (0, 0))
    cache = pl.BlockSpec((None, None, PAST_LEN, ATTN_W), lambda b, i: (b, layer, 0, 0))

    def kv(col):
        return pl.BlockSpec((DEC_SEQ, ATTN_W), lambda b, i: (P_ROWS // DEC_SEQ + b, col // ATTN_W))

    args = [proj, proj, proj, cache_k, cache_v, *rope, lam_q1, lam_k1, lam_q2, lam_k2, subln_g]
    return pl.pallas_call(
        functools.partial(_attn_lat_kernel, lam_init=lam_init),
        grid=(DEC_BATCH, nq),
        in_specs=[pl.BlockSpec((tq, ATTN_W), lambda b, i: (P_ROWS // tq + b * nq + i, COL_Q // ATTN_W)),
                  kv(COL_K), kv(COL_V), cache, cache, qtab, qtab, qtab, ktab, ktab, ktab,
                  lv, lv, lv, lv, _layer_vec_spec(V_DIM, layer)],
        out_specs=pl.BlockSpec((tq, ATTN_W), lambda b, i: (b * nq + i, 0)),
        out_shape=jax.ShapeDtypeStruct((S_ROWS, ATTN_W), BF16),
        scratch_shapes=[pltpu.VMEM((ATTN_W, PAST_LEN + DEC_SEQ), BF16),
                        pltpu.VMEM((PAST_LEN + DEC_SEQ, ATTN_W), BF16)],
        compiler_params=_params(2),
        name="attn_lat",
    )(*args)


def kernel(x_prompt, x_sample, cache_k, cache_v, c, c_ctx, w_ada, b_ada, g_pre_mix, g_post_mix,
           g_pre_ffn, g_post_ffn, w_in, conv_a_w, conv_a_b, ln_a_g, ln_a_b, w_a_out,
           lam_q1, lam_k1, lam_q2, lam_k2, subln_g, w_b_out, conv_c_w, w_c_out, w_d_out,
           w_gate, b_gate, w_o, w_ffn_up, w_ffn_down):
    x = jnp.concatenate([x_prompt.reshape(P_ROWS, D_MODEL), x_sample.reshape(S_ROWS, D_MODEL)], axis=0)
    cs = jnp.concatenate([c_ctx[None, :], c, jnp.zeros((ADA_ROWS - N_GROUPS, D_MODEL), F32)], axis=0)
    mod = _ada(cs, w_ada, b_ada)[:, :N_GROUPS].reshape(DEPTH * N_GROUPS * N_MOD, 1, D_MODEL)

    vec = lambda a: a.reshape(DEPTH, 1, a.shape[-1])
    g_pre_mix, g_post_mix, g_pre_ffn, g_post_ffn = map(vec, (g_pre_mix, g_post_mix, g_pre_ffn, g_post_ffn))
    conv_a_b, ln_a_g, ln_a_b, subln_g, b_gate = map(vec, (conv_a_b, ln_a_g, ln_a_b, subln_g, b_gate))
    lam_q1, lam_k1, lam_q2, lam_k2 = map(vec, (lam_q1, lam_k1, lam_q2, lam_k2))
    cache_k = cache_k.reshape(DEC_BATCH, DEPTH, PAST_LEN, ATTN_W)
    cache_v = cache_v.reshape(DEC_BATCH, DEPTH, PAST_LEN, ATTN_W)
    rope = _rope_tables()
    dft = _fourier_tables()

    def both(fn, *args):
        return fn(*args, SEQ, BATCH, 0), fn(*args, DEC_SEQ, DEC_BATCH, P_ROWS)

    h = _normmod(x, g_pre_mix, mod, 0, 1, 0)
    ks_new, vs_new = [], []
    for l in range(DEPTH):
        proj = _mm(h, w_in, l, tm=1024, tn=1024, out_dtype=F32, name="mm_in")
        gates = _mm(h, w_gate, l, tm=1024, tn=1024, out_dtype=BF16, bias=b_gate, name="mm_gate")

        ya = both(_conv_a, proj, conv_a_w, conv_a_b, ln_a_g, ln_a_b, l)
        yc = both(_conv_c, proj, conv_c_w, l)
        yd = both(_fourier, proj, dft)
        lam_args = (lam_q1, lam_k1, lam_q2, lam_k2, subln_g, l)
        yb_ctx, k_new, v_new = _attn_ctx(proj, *lam_args)
        yb = (yb_ctx, _attn_lat(proj, cache_k, cache_v, rope, *lam_args))
        ks_new.append(k_new.reshape(BATCH, SEQ, N_HEADS, V_DIM))
        vs_new.append(v_new.reshape(BATCH, SEQ, N_HEADS, V_DIM))

        merged = _merge((ya, yb, yc, yd), gates, (w_a_out, w_b_out, w_c_out, w_d_out), l)
        x, h2 = _mm_resid(merged, w_o, x, mod, l, 2, g_post_mix, (g_pre_ffn, l, 4, 3),
                          tm=512, piece=128, chunk=512, name="mm_o")
        act = _ffn_up(h2, w_ffn_up, l)
        nxt = (g_pre_mix, l + 1, 1, 0) if l + 1 < DEPTH else None
        res = _mm_resid(act, w_ffn_down, x, mod, l, 5, g_post_ffn, nxt,
                        tm=256, piece=128, chunk=512, name="mm_down")
        if nxt is not None:
            x, h = res

    y_ctx, y_lat = res
    return (y_ctx.reshape(BATCH, SEQ, D_MODEL), y_lat.reshape(DEC_BATCH, DEC_SEQ, D_MODEL),
            jnp.stack(ks_new, axis=1), jnp.stack(vs_new, axis=1))
```

```python
import functools
import math

import numpy as np
import jax
import jax.numpy as jnp
from jax import lax
from jax.experimental import pallas as pl
from jax.experimental.pallas import tpu as pltpu

D_MODEL = 2048
BATCH = 16
SEQ = 256
DEPTH = 4
DEC_BATCH = 2
DEC_SEQ = 1024
PAST_LEN = 512
GRID_W = 64
N_HEADS = 4
HEAD_DIM = 128
V_DIM = 2 * HEAD_DIM
ATTN_W = N_HEADS * 2 * HEAD_DIM
CONV_W = 512
CONV_K = 31
SC_W = 512
SC_K = 3
F_W = 512
F_GROUPS = 4
F_GROUP = F_W // F_GROUPS
N_BRANCH = 4
D_IN = 2 * CONV_W + 3 * ATTN_W + 3 * SC_W + F_W
D_FF = ((8 * D_MODEL + 3 * 256 - 1) // (3 * 256)) * 256
ROPE_THETA = 10000.0
EPS = 1e-6

P_ROWS = BATCH * SEQ
S_ROWS = DEC_BATCH * DEC_SEQ
N_ROWS = P_ROWS + S_ROWS
N_GROUPS = 1 + DEC_BATCH
N_MOD = 6
ADA_ROWS = 8

COL_A = 0
COL_Q = 2 * CONV_W
COL_K = COL_Q + ATTN_W
COL_V = COL_K + ATTN_W
COL_SB = COL_V + ATTN_W
COL_SC = COL_SB + SC_W
COL_SX = COL_SC + SC_W
COL_D = COL_SX + SC_W

SUBLANES = 8
CONV_PAD = 16
CONV_ROWS = 64
VMEM_LIMIT = 56 * 1024 * 1024
MM_PIECE = 256

F32 = jnp.float32
BF16 = jnp.bfloat16


def _params(n_axes, vmem=VMEM_LIMIT):
    return pltpu.CompilerParams(dimension_semantics=("arbitrary",) * n_axes, vmem_limit_bytes=vmem)


def _group_of_tile(i, tm):
    pt = P_ROWS // tm
    st = DEC_SEQ // tm
    return jnp.maximum(i - (pt - st), 0) // st


def _mod_spec(layer, which, tm, tile_of):
    def index(*ids):
        g = _group_of_tile(tile_of(*ids), tm)
        return (layer * N_GROUPS * N_MOD + g * N_MOD + which, 0, 0)
    return pl.BlockSpec((None, 1, D_MODEL), index)


def _layer_vec_spec(width, layer):
    return pl.BlockSpec((None, 1, width), lambda *ids: (layer, 0, 0))


def _ada_kernel(c_ref, w_ref, b_ref, o_ref):
    s = _silu(c_ref[...]).astype(BF16)
    o_ref[...] = jnp.dot(s, w_ref[...].astype(BF16), preferred_element_type=F32) + b_ref[...]


def _ada(cs, w_ada, b_ada):
    tn = 1024
    n = N_MOD * D_MODEL
    return pl.pallas_call(
        _ada_kernel,
        grid=(DEPTH, n // tn),
        in_specs=[
            pl.BlockSpec((ADA_ROWS, D_MODEL), lambda l, j: (0, 0)),
            pl.BlockSpec((None, D_MODEL, tn), lambda l, j: (l, 0, j)),
            pl.BlockSpec((None, 1, tn), lambda l, j: (l, 0, j)),
        ],
        out_specs=pl.BlockSpec((None, ADA_ROWS, tn), lambda l, j: (l, 0, j)),
        out_shape=jax.ShapeDtypeStruct((DEPTH, ADA_ROWS, n), F32),
        compiler_params=_params(2),
        name="ada",
    )(cs, w_ada, b_ada.reshape(DEPTH, 1, n))


def _rms(x):
    return x * lax.rsqrt(jnp.mean(x * x, axis=-1, keepdims=True) + EPS)


def _sigmoid(x):
    return 0.5 * jnp.tanh(0.5 * x) + 0.5


def _silu(x):
    return x * _sigmoid(x)


def _normmod_kernel(xc_ref, xl_ref, g_ref, sc_ref, sh_ref, x_ref, h_ref, *, ctx_tiles):
    def tile(src_ref):
        x = src_ref[...]
        x_ref[...] = x
        y = _rms(x) * g_ref[...]
        h_ref[...] = (y * (1.0 + sc_ref[...]) + sh_ref[...]).astype(BF16)

    is_ctx = pl.program_id(0) < ctx_tiles
    pl.when(is_ctx)(lambda: tile(xc_ref))
    pl.when(jnp.logical_not(is_ctx))(lambda: tile(xl_ref))


def _normmod(x_ctx, x_lat, g, mod, layer, which_sc, which_sh):
    tm = 256
    ctx_tiles = P_ROWS // tm
    row = pl.BlockSpec((tm, D_MODEL), lambda i: (i, 0))
    tile = lambda i: i
    return pl.pallas_call(
        functools.partial(_normmod_kernel, ctx_tiles=ctx_tiles),
        grid=(N_ROWS // tm,),
        in_specs=[pl.BlockSpec((tm, D_MODEL), lambda i: (jnp.minimum(i, ctx_tiles - 1), 0)),
                  pl.BlockSpec((tm, D_MODEL), lambda i: (jnp.maximum(i - ctx_tiles, 0), 0)),
                  _layer_vec_spec(D_MODEL, layer),
                  _mod_spec(layer, which_sc, tm, tile), _mod_spec(layer, which_sh, tm, tile)],
        out_specs=[row, row],
        out_shape=[jax.ShapeDtypeStruct((N_ROWS, D_MODEL), F32), jax.ShapeDtypeStruct((N_ROWS, D_MODEL), BF16)],
        compiler_params=_params(1),
        name="normmod",
    )(x_ctx, x_lat, g, mod, mod)


def _cast_weight(w_ref, wbf_ref):
    rows = w_ref.shape[0]
    chunk = 256

    def body(r, carry):
        r0 = pl.multiple_of(r * chunk, chunk)
        wbf_ref[pl.ds(r0, chunk), :] = w_ref[pl.ds(r0, chunk), :].astype(BF16)
        return carry

    lax.fori_loop(0, rows // chunk, body, 0)


def _mm_kernel(x_ref, w_ref, *rest, sigmoid_bias):
    if sigmoid_bias:
        b_ref, o_ref, wbf_ref = rest
    else:
        o_ref, wbf_ref = rest

    @pl.when(pl.program_id(1) == 0)
    def _():
        _cast_weight(w_ref, wbf_ref)

    x = x_ref[...]
    for c in range(0, o_ref.shape[1], MM_PIECE):
        cols = slice(c, c + MM_PIECE)
        acc = jnp.dot(x, wbf_ref[:, cols], preferred_element_type=F32)
        if sigmoid_bias:
            acc = _sigmoid(acc + b_ref[:, cols])
        o_ref[:, cols] = acc.astype(o_ref.dtype)


def _mm(x, w, layer, *, tm, tn, out_dtype, bias=None, name):
    m, k = x.shape
    n = w.shape[2]
    in_specs = [pl.BlockSpec((tm, k), lambda j, i: (i, 0)),
                pl.BlockSpec((None, k, tn), lambda j, i: (layer, 0, j))]
    args = [x, w]
    if bias is not None:
        in_specs.append(pl.BlockSpec((None, 1, tn), lambda j, i: (layer, 0, j)))
        args.append(bias)
    return pl.pallas_call(
        functools.partial(_mm_kernel, sigmoid_bias=bias is not None),
        grid=(n // tn, m // tm),
        in_specs=in_specs,
        out_specs=pl.BlockSpec((tm, tn), lambda j, i: (i, j)),
        out_shape=jax.ShapeDtypeStruct((m, n), out_dtype),
        scratch_shapes=[pltpu.VMEM((k, tn), BF16)],
        compiler_params=_params(2),
        name=name,
    )(*args)


def _merge_kernel(*refs, ctx_tiles):
    ctx_branches, lat_branches = refs[0:N_BRANCH], refs[N_BRANCH:2 * N_BRANCH]
    gate_refs = refs[2 * N_BRANCH:3 * N_BRANCH]
    w_refs = refs[3 * N_BRANCH:4 * N_BRANCH]
    o_ref = refs[4 * N_BRANCH]
    w_scratch = refs[4 * N_BRANCH + 1:]

    @pl.when(pl.program_id(1) == 0)
    def _():
        for w_ref, w_s in zip(w_refs, w_scratch):
            _cast_weight(w_ref, w_s)

    def merge(branches):
        acc = None
        for y_ref, g_ref, w_s in zip(branches, gate_refs, w_scratch):
            term = g_ref[...].astype(F32) * jnp.dot(y_ref[...], w_s[...], preferred_element_type=F32)
            acc = term if acc is None else acc + term
        o_ref[...] = acc.astype(BF16)

    is_ctx = pl.program_id(1) < ctx_tiles
    pl.when(is_ctx)(lambda: merge(ctx_branches))
    pl.when(jnp.logical_not(is_ctx))(lambda: merge(lat_branches))


def _merge(branches, gates, weights, layer):
    tm, tn = 512, 1024
    nt = D_MODEL // tn
    ctx_tiles = P_ROWS // tm
    widths = [w.shape[1] for w in weights]

    def ctx(width):
        return pl.BlockSpec((tm, width), lambda j, i: (jnp.minimum(i, ctx_tiles - 1), 0))

    def lat(width):
        return pl.BlockSpec((tm, width), lambda j, i: (jnp.maximum(i - ctx_tiles, 0), 0))

    def gate(branch):
        return pl.BlockSpec((tm, tn), lambda j, i: (i, branch * nt + j))

    def wt(width):
        return pl.BlockSpec((None, width, tn), lambda j, i: (layer, 0, j))

    return pl.pallas_call(
        functools.partial(_merge_kernel, ctx_tiles=ctx_tiles),
        grid=(nt, N_ROWS // tm),
        in_specs=([ctx(w) for w in widths] + [lat(w) for w in widths]
                  + [gate(b) for b in range(N_BRANCH)] + [wt(w) for w in widths]),
        out_specs=pl.BlockSpec((tm, tn), lambda j, i: (i, j)),
        out_shape=jax.ShapeDtypeStruct((N_ROWS, D_MODEL), BF16),
        scratch_shapes=[pltpu.VMEM((w, tn), BF16) for w in widths],
        compiler_params=_params(2),
        name="merge",
    )(*[b[0] for b in branches], *[b[1] for b in branches], *([gates] * N_BRANCH), *weights)


def _ffn_up_kernel(x_ref, wu_ref, wg_ref, o_ref, wu_s, wg_s):
    @pl.when(pl.program_id(1) == 0)
    def _():
        _cast_weight(wu_ref, wu_s)
        _cast_weight(wg_ref, wg_s)

    x = x_ref[...]
    for c in range(0, o_ref.shape[1], MM_PIECE):
        cols = slice(c, c + MM_PIECE)
        up = jnp.dot(x, wu_s[:, cols], preferred_element_type=F32)
        gt = jnp.dot(x, wg_s[:, cols], preferred_element_type=F32)
        o_ref[:, cols] = (_silu(gt) * up).astype(BF16)


def _ffn_up(h, w_up, layer):
    tm, tn = 2048, 512
    nt = D_FF // tn
    return pl.pallas_call(
        _ffn_up_kernel,
        grid=(nt, N_ROWS // tm),
        in_specs=[pl.BlockSpec((tm, D_MODEL), lambda j, i: (i, 0)),
                  pl.BlockSpec((None, D_MODEL, tn), lambda j, i: (layer, 0, j)),
                  pl.BlockSpec((None, D_MODEL, tn), lambda j, i: (layer, 0, nt + j))],
        out_specs=pl.BlockSpec((tm, tn), lambda j, i: (i, j)),
        out_shape=jax.ShapeDtypeStruct((N_ROWS, D_FF), BF16),
        scratch_shapes=[pltpu.VMEM((D_MODEL, tn), BF16), pltpu.VMEM((D_MODEL, tn), BF16)],
        compiler_params=_params(2),
        name="ffn_up",
    )(h, w_up, w_up)


EPI_ROWS = 16


def _mm_resid_kernel(*refs, nw, chunk, tm, piece, with_next):
    if with_next:
        a_ref, w_ref, x_ref, gate_ref, gpost_ref, gpre_ref, sc_ref, sh_ref, xo_ref, h_ref, wbf_ref = refs
    else:
        a_ref, w_ref, x_ref, gate_ref, gpost_ref, yctx_ref, ylat_ref, wbf_ref = refs
    s = pl.program_id(0)

    @pl.when(s < nw)
    def _():
        r0 = pl.multiple_of(s * chunk, chunk)
        wbf_ref[pl.ds(r0, chunk), :] = w_ref[...].astype(BF16)

    def row_tile(xo_ref):
        gg = gate_ref[...] * gpost_ref[...]
        if with_next:
            pp = gpre_ref[...] * (1.0 + sc_ref[...])
            sh = sh_ref[...]
        for p0 in range(0, tm, piece):
            f = jnp.dot(a_ref[p0:p0 + piece, :], wbf_ref[...], preferred_element_type=F32)
            for r in range(0, piece, EPI_ROWS):
                rows = slice(p0 + r, p0 + r + EPI_ROWS)
                xn = x_ref[rows, :] + _rms(f[r:r + EPI_ROWS, :]) * gg
                xo_ref[rows, :] = xn
                if with_next:
                    h_ref[rows, :] = (_rms(xn) * pp + sh).astype(BF16)

    if with_next:
        pl.when(s >= nw)(lambda: row_tile(xo_ref))
    else:
        first_lat = nw + P_ROWS // tm
        pl.when((s >= nw) & (s < first_lat))(lambda: row_tile(yctx_ref))
        pl.when(s >= first_lat)(lambda: row_tile(ylat_ref))


def _mm_resid(a, w, x, mod, layer, which_gate, g_post, nxt, *, tm, piece, chunk, name):
    k = a.shape[1]
    nw = k // chunk
    nt = N_ROWS // tm
    ctx_tiles = P_ROWS // tm
    tile = lambda s: jnp.maximum(s - nw, 0)
    row = pl.BlockSpec((tm, D_MODEL), lambda s: (tile(s), 0))
    in_specs = [pl.BlockSpec((tm, k), lambda s: (tile(s), 0)),
                pl.BlockSpec((None, chunk, D_MODEL), lambda s: (layer, jnp.minimum(s, nw - 1), 0)),
                row, _mod_spec(layer, which_gate, tm, tile), _layer_vec_spec(D_MODEL, layer)]
    args = [a, w, x, mod, g_post]
    if nxt is not None:
        g_pre, nl, wsc, wsh = nxt
        in_specs += [_layer_vec_spec(D_MODEL, nl), _mod_spec(nl, wsc, tm, tile), _mod_spec(nl, wsh, tm, tile)]
        args += [g_pre, mod, mod]
        out_specs = [row, row]
        out_shape = [jax.ShapeDtypeStruct((N_ROWS, D_MODEL), F32), jax.ShapeDtypeStruct((N_ROWS, D_MODEL), BF16)]
    else:
        out_specs = [pl.BlockSpec((tm, D_MODEL), lambda s: (jnp.minimum(tile(s), ctx_tiles - 1), 0)),
                     pl.BlockSpec((tm, D_MODEL), lambda s: (jnp.maximum(tile(s) - ctx_tiles, 0), 0))]
        out_shape = [jax.ShapeDtypeStruct((P_ROWS, D_MODEL), F32), jax.ShapeDtypeStruct((S_ROWS, D_MODEL), F32)]
    return pl.pallas_call(
        functools.partial(_mm_resid_kernel, nw=nw, chunk=chunk, tm=tm, piece=piece, with_next=nxt is not None),
        grid=(nw + nt,),
        in_specs=in_specs,
        out_specs=out_specs,
        out_shape=out_shape,
        scratch_shapes=[pltpu.VMEM((k, D_MODEL), BF16)],
        compiler_params=_params(1),
        name=name,
    )(*args)


def _seq_spec(t, width, col, row0):
    return pl.BlockSpec((t, width), lambda b: (row0 // t + b, col // width))


def _seq_call(kernel_fn, name, t, nseq, in_specs, args, width, scratch):
    return pl.pallas_call(
        kernel_fn,
        grid=(nseq,),
        in_specs=in_specs,
        out_specs=pl.BlockSpec((t, width), lambda b: (b, 0)),
        out_shape=jax.ShapeDtypeStruct((nseq * t, width), BF16),
        scratch_shapes=scratch,
        compiler_params=_params(1),
        name=name,
    )(*args)


def _dwconv_chunk(pad_ref, w_ref, r0, ksize):
    win = CONV_ROWS + 2 * CONV_PAD
    window = pad_ref[r0:r0 + win, :]
    acc = None
    for b in range(SUBLANES):
        taps = [j for j in range(ksize) if (CONV_PAD - ksize // 2 + j) % SUBLANES == b]
        if not taps:
            continue
        shifted = window if b == 0 else pltpu.roll(window, win - b, 0)
        for j in taps:
            a0 = (CONV_PAD - ksize // 2 + j) - b
            term = w_ref[j:j + 1, :] * shifted[a0:a0 + CONV_ROWS, :]
            acc = term if acc is None else acc + term
    return acc


def _fill_pad(pad_ref, u, t):
    zeros = jnp.zeros((CONV_PAD, pad_ref.shape[1]), F32)
    pad_ref[0:CONV_PAD, :] = zeros
    pad_ref[CONV_PAD + t:2 * CONV_PAD + t, :] = zeros
    pad_ref[CONV_PAD:CONV_PAD + t, :] = u


def _conv_a_kernel(a_ref, w_ref, cb_ref, lg_ref, lb_ref, o_ref, pad_ref, *, t):
    _fill_pad(pad_ref, a_ref[:, 0:CONV_W] * _sigmoid(a_ref[:, CONV_W:2 * CONV_W]), t)
    for r0 in range(0, t, CONV_ROWS):
        acc = _dwconv_chunk(pad_ref, w_ref, r0, CONV_K) + cb_ref[...]
        mu = jnp.mean(acc, axis=-1, keepdims=True)
        xc = acc - mu
        y = xc * lax.rsqrt(jnp.mean(xc * xc, axis=-1, keepdims=True) + EPS)
        y = y * lg_ref[...] + lb_ref[...]
        o_ref[r0:r0 + CONV_ROWS, :] = _silu(y).astype(BF16)


def _conv_a(proj, conv_w, conv_b, ln_g, ln_b, layer, t, nseq, row0):
    in_specs = [_seq_spec(t, 2 * CONV_W, COL_A, row0),
                pl.BlockSpec((None, CONV_K, CONV_W), lambda b: (layer, 0, 0)),
                _layer_vec_spec(CONV_W, layer), _layer_vec_spec(CONV_W, layer),
                _layer_vec_spec(CONV_W, layer)]
    return _seq_call(functools.partial(_conv_a_kernel, t=t), "conv_a", t, nseq, in_specs,
                     [proj, conv_w, conv_b, ln_g, ln_b], CONV_W,
                     [pltpu.VMEM((t + 2 * CONV_PAD, CONV_W), F32)])


def _conv_c_kernel(sb_ref, sc_ref, sx_ref, w_ref, o_ref, pad_ref, *, t):
    _fill_pad(pad_ref, sc_ref[...] * sx_ref[...], t)
    for r0 in range(0, t, CONV_ROWS):
        acc = _dwconv_chunk(pad_ref, w_ref, r0, SC_K)
        o_ref[r0:r0 + CONV_ROWS, :] = (sb_ref[r0:r0 + CONV_ROWS, :] * acc).astype(BF16)


def _conv_c(proj, conv_w, layer, t, nseq, row0):
    in_specs = [_seq_spec(t, SC_W, COL_SB, row0), _seq_spec(t, SC_W, COL_SC, row0),
                _seq_spec(t, SC_W, COL_SX, row0),
                pl.BlockSpec((None, SC_K, SC_W), lambda b: (layer, 0, 0))]
    return _seq_call(functools.partial(_conv_c_kernel, t=t), "conv_c", t, nseq, in_specs,
                     [proj, proj, proj, conv_w], SC_W,
                     [pltpu.VMEM((t + 2 * CONV_PAD, SC_W), F32)])


def _fourier_kernel(u_ref, cc_ref, sc_ref, ct_ref, st_ref, o_ref):
    u =u_ref[...].astype(BF16)
    yc = jnp.dot(u, cc_ref[...], preferred_element_type=F32).astype(BF16)
    ys = jnp.dot(u, sc_ref[...], preferred_element_type=F32).astype(BF16)
    f = (jnp.dot(ct_ref[...], yc, preferred_element_type=F32)
         - jnp.dot(st_ref[...], ys, preferred_element_type=F32))
    o_ref[...] = f.astype(BF16)


def _dft_tables(n):
    idx = np.arange(n)
    ang = 2.0 * np.pi * ((idx[:, None] * idx[None, :]) % n) / n
    s = 1.0 / math.sqrt(n)
    return (np.cos(ang) * s).astype(np.float32), (np.sin(ang) * s).astype(np.float32)


def _fourier_tables():
    cg, sg = _dft_tables(F_GROUP)
    eye = np.eye(F_GROUPS, dtype=np.float32)
    tabs = {"chan": (np.kron(eye, cg), np.kron(eye, sg)), SEQ: _dft_tables(SEQ), DEC_SEQ: _dft_tables(DEC_SEQ)}
    return {k: tuple(jnp.asarray(m).astype(BF16) for m in v) for k, v in tabs.items()}


def _fourier(proj, tabs, t, nseq, row0):
    full = lambda n: pl.BlockSpec((n, n), lambda b: (0, 0))
    in_specs = [_seq_spec(t, F_W, COL_D, row0), full(F_W), full(F_W), full(t), full(t)]
    return _seq_call(_fourier_kernel, "fourier", t, nseq, in_specs,
                     [proj, *tabs["chan"], *tabs[t]], F_W, [])


def _lam_of(lq1_ref, lk1_ref, lq2_ref, lk2_ref, lam_init):
    s1 = jnp.sum(lq1_ref[...] * lk1_ref[...], axis=-1, keepdims=True)
    s2 = jnp.sum(lq2_ref[...] * lk2_ref[...], axis=-1, keepdims=True)
    return jnp.exp(s1) - jnp.exp(s2) + lam_init


LOGIT_SCALE = HEAD_DIM ** -0.5 * math.log2(math.e)


def _exp2_and_rsum(t):
    e = jnp.exp2(t - jnp.max(t, axis=-1, keepdims=True))
    return e, 1.0 / jnp.sum(e, axis=-1, keepdims=True)


def _transposed_chunks(k):
    return [k[:, c * HEAD_DIM:(c + 1) * HEAD_DIM].T.astype(BF16) for c in range(ATTN_W // HEAD_DIM)]


def _diff_attn_heads(q, kt_of, v_of, lam, subln, lam_init, o_ref):
    def logits(h):
        maps = []
        for m in range(2):
            c = 2 * h + m
            qc = (q[:, c * HEAD_DIM:(c + 1) * HEAD_DIM] * LOGIT_SCALE).astype(BF16)
            maps.append(jnp.dot(qc, kt_of(c), preferred_element_type=F32))
        return maps

    pending = logits(0)
    for h in range(N_HEADS):
        following = logits(h + 1) if h + 1 < N_HEADS else None
        (e1, r1), (e2, r2) = (_exp2_and_rsum(t) for t in pending)
        a = (e1 * r1 - e2 * (lam * r2)).astype(BF16)
        o = jnp.dot(a, v_of(h), preferred_element_type=F32)
        o = _rms(o) * subln * (1.0 - lam_init)
        o_ref[:, h * V_DIM:(h + 1) * V_DIM] = o.astype(BF16)
        pending = following


def _attn_ctx_kernel(q_ref, k_ref, v_ref, lq1, lk1, lq2, lk2, sg_ref, nk_in, nv_in, o_ref, nk_ref, nv_ref, *,
                     lam_init):
    lam = _lam_of(lq1, lk1, lq2, lk2, lam_init)
    k = k_ref[...]
    v = v_ref[...]
    for h in range(N_HEADS):
        nk_ref[:, h, :] = k[:, h * V_DIM:(h + 1) * V_DIM]
        nv_ref[:, h, :] = v[:, h * V_DIM:(h + 1) * V_DIM]
    kt =_transposed_chunks(k)
    vb = v.astype(BF16)
    _diff_attn_heads(q_ref[...],
                     lambda c: kt[c],
                     lambda h: vb[:, h * V_DIM:(h + 1) * V_DIM],
                     lam, sg_ref[...], lam_init, o_ref)


def _attn_ctx(proj, lam_q1, lam_k1, lam_q2, lam_k2, subln_g, layer, new_k, new_v):
    lam_init = 0.8 - 0.6 * math.exp(-0.3 * layer)
    lv = _layer_vec_spec(HEAD_DIM, layer)
    hbm = pl.BlockSpec(memory_space=pl.ANY)
    in_specs = [_seq_spec(SEQ, ATTN_W, COL_Q, 0), _seq_spec(SEQ, ATTN_W, COL_K, 0),
                _seq_spec(SEQ, ATTN_W, COL_V, 0), lv, lv, lv, lv, _layer_vec_spec(V_DIM, layer), hbm, hbm]
    args = [proj, proj, proj, lam_q1, lam_k1, lam_q2, lam_k2, subln_g, new_k, new_v]
    kv_spec = pl.BlockSpec((None, None, SEQ, N_HEADS, V_DIM), lambda b: (b, layer, 0, 0, 0))
    kv_shape = jax.ShapeDtypeStruct(new_k.shape, F32)
    return pl.pallas_call(
        functools.partial(_attn_ctx_kernel, lam_init=lam_init),
        grid=(BATCH,),
        in_specs=in_specs,
        out_specs=[pl.BlockSpec((SEQ, ATTN_W), lambda b: (b, 0)), kv_spec, kv_spec],
        out_shape=[jax.ShapeDtypeStruct((P_ROWS, ATTN_W), BF16), kv_shape, kv_shape],
        input_output_aliases={len(args) - 2: 1, len(args) - 1: 2},
        compiler_params=_params(1),
        name="attn_ctx",
    )(*args)


def _rope(x, cos, sin_a, sin_b):
    outs = []
    for c in range(ATTN_W // HEAD_DIM):
        xc = x[:, c * HEAD_DIM:(c + 1) * HEAD_DIM]
        fwd = pltpu.roll(xc, HEAD_DIM - HEAD_DIM // 4, 1)
        bwd = pltpu.roll(xc, HEAD_DIM // 4, 1)
        outs.append(xc * cos + fwd * sin_a + bwd * sin_b)
    return outs


def _attn_lat_kernel(q_ref, k_ref, v_ref, ck_ref, cv_ref, cosq_ref, sinaq_ref, sinbq_ref,
                     cos_ref, sina_ref, sinb_ref, lq1, lk1, lq2, lk2, sg_ref, o_ref,
                     kt_ref, vall_ref, *, lam_init):
    @pl.when(pl.program_id(1) == 0)
    def _():
        vall_ref[0:PAST_LEN, :] = cv_ref[...].astype(BF16)
        vall_ref[PAST_LEN:PAST_LEN + DEC_SEQ, :] = v_ref[...].astype(BF16)
        for c, kc in enumerate(_transposed_chunks(ck_ref[...])):
            kt_ref[c * HEAD_DIM:(c + 1) * HEAD_DIM, 0:PAST_LEN] = kc
        for c in range(ATTN_W // HEAD_DIM):
            kt = k_ref[:, c * HEAD_DIM:(c + 1) * HEAD_DIM].T
            fwd = pltpu.roll(kt, HEAD_DIM - HEAD_DIM // 4, 0)
            bwd = pltpu.roll(kt, HEAD_DIM // 4, 0)
            kr = kt * cos_ref[...] + fwd * sina_ref[...] + bwd * sinb_ref[...]
            kt_ref[c * HEAD_DIM:(c + 1) * HEAD_DIM, PAST_LEN:PAST_LEN + DEC_SEQ] = kr.astype(BF16)

    lam = _lam_of(lq1, lk1, lq2, lk2, lam_init)
    q = jnp.concatenate(_rope(q_ref[...], cosq_ref[...], sinaq_ref[...], sinbq_ref[...]), axis=-1)
    _diff_attn_heads(q,
                     lambda c: kt_ref[c * HEAD_DIM:(c + 1) * HEAD_DIM, :],
                     lambda h: vall_ref[:, h * V_DIM:(h + 1) * V_DIM],
                     lam, sg_ref[...], lam_init, o_ref)


def _rope_tables():
    t = jnp.arange(DEC_SEQ)
    row = (t // GRID_W).astype(F32)
    col = (t % GRID_W).astype(F32)
    half = HEAD_DIM // 2
    inv = ROPE_THETA ** (-jnp.arange(0, half, 2, dtype=F32) / half)
    ar = row[:, None] * inv
    ac = col[:, None] * inv
    ang = jnp.concatenate([ar, ar, ac, ac], axis=-1)
    cos, sin = jnp.cos(ang), jnp.sin(ang)
    first = (np.arange(HEAD_DIM) % half) < (half // 2)
    sin_a = jnp.where(first, -sin, 0.0)
    sin_b = jnp.where(first, 0.0, sin)
    return cos, sin_a, sin_b, cos.T, sin_a.T, sin_b.T


def _attn_lat(proj, cache_k, cache_v, rope, lam_q1, lam_k1, lam_q2, lam_k2, subln_g, layer):
    lam_init = 0.8 - 0.6 * math.exp(-0.3 * layer)
    tq = 256
    nq = DEC_SEQ // tq
    lv = _layer_vec_spec(HEAD_DIM, layer)
    qtab = pl.BlockSpec((tq, HEAD_DIM), lambda b, i: (i, 0))
    ktab = pl.BlockSpec((HEAD_DIM, DEC_SEQ), lambda b, i: (0, 0))
    cache = pl.BlockSpec((None, None, PAST_LEN, ATTN_W), lambda b, i: (b, layer, 0, 0))

    def kv(col):
        return pl.BlockSpec((DEC_SEQ, ATTN_W), lambda b, i: (P_ROWS // DEC_SEQ + b, col // ATTN_W))

    args = [proj, proj, proj, cache_k, cache_v, *rope, lam_q1, lam_k1, lam_q2, lam_k2, subln_g]
    return pl.pallas_call(
        functools.partial(_attn_lat_kernel, lam_init=lam_init),
        grid=(DEC_BATCH, nq),
        in_specs=[pl.BlockSpec((tq, ATTN_W), lambda b, i: (P_ROWS // tq + b * nq + i, COL_Q // ATTN_W)),
                  kv(COL_K), kv(COL_V), cache, cache, qtab, qtab, qtab, ktab, ktab, ktab,
                  lv, lv, lv, lv, _layer_vec_spec(V_DIM, layer)],
        out_specs=pl.BlockSpec((tq, ATTN_W), lambda b, i: (b * nq + i, 0)),
        out_shape=jax.ShapeDtypeStruct((S_ROWS, ATTN_W), BF16),
        scratch_shapes=[pltpu.VMEM((ATTN_W, PAST_LEN + DEC_SEQ), BF16),
                        pltpu.VMEM((PAST_LEN + DEC_SEQ, ATTN_W), BF16)],
        compiler_params=_params(2),
        name="attn_lat",
    )(*args)


def kernel(x_prompt, x_sample, cache_k, cache_v, c, c_ctx, w_ada, b_ada, g_pre_mix, g_post_mix,
           g_pre_ffn, g_post_ffn, w_in, conv_a_w, conv_a_b, ln_a_g, ln_a_b, w_a_out,
           lam_q1, lam_k1, lam_q2, lam_k2, subln_g, w_b_out, conv_c_w, w_c_out, w_d_out,
           w_gate, b_gate, w_o, w_ffn_up, w_ffn_down):
    cs = jnp.concatenate([c_ctx[None, :], c, jnp.zeros((ADA_ROWS - N_GROUPS, D_MODEL), F32)], axis=0)
    mod = _ada(cs, w_ada, b_ada)[:, :N_GROUPS].reshape(DEPTH * N_GROUPS * N_MOD, 1, D_MODEL)

    vec = lambda a: a.reshape(DEPTH, 1, a.shape[-1])
    g_pre_mix, g_post_mix, g_pre_ffn, g_post_ffn = map(vec, (g_pre_mix, g_post_mix, g_pre_ffn, g_post_ffn))
    conv_a_b, ln_a_g, ln_a_b, subln_g, b_gate = map(vec, (conv_a_b, ln_a_g, ln_a_b, subln_g, b_gate))
    lam_q1, lam_k1, lam_q2, lam_k2 = map(vec, (lam_q1, lam_k1, lam_q2, lam_k2))
    cache_k = cache_k.reshape(DEC_BATCH, DEPTH, PAST_LEN, ATTN_W)
    cache_v = cache_v.reshape(DEC_BATCH, DEPTH, PAST_LEN, ATTN_W)
    rope = _rope_tables()
    dft = _fourier_tables()

    def both(fn, *args):
        return fn(*args, SEQ, BATCH, 0), fn(*args, DEC_SEQ, DEC_BATCH, P_ROWS)

    x, h = _normmod(x_prompt.reshape(P_ROWS, D_MODEL), x_sample.reshape(S_ROWS, D_MODEL),
                    g_pre_mix, mod, 0, 1, 0)
    new_k = jnp.zeros((BATCH, DEPTH, SEQ, N_HEADS, V_DIM), F32)
    new_v = jnp.zeros((BATCH, DEPTH, SEQ, N_HEADS, V_DIM), F32)
    for l in range(DEPTH):
        proj = _mm(h, w_in, l, tm=1024, tn=1024, out_dtype=F32, name="mm_in")
        gates = _mm(h, w_gate, l, tm=1024, tn=1024, out_dtype=BF16, bias=b_gate, name="mm_gate")

        ya = both(_conv_a, proj, conv_a_w, conv_a_b, ln_a_g, ln_a_b, l)
        yc = both(_conv_c, proj, conv_c_w, l)
        yd = both(_fourier, proj, dft)
        lam_args = (lam_q1, lam_k1, lam_q2, lam_k2, subln_g, l)
        yb_ctx, new_k, new_v = _attn_ctx(proj, *lam_args, new_k, new_v)
        yb = (yb_ctx, _attn_lat(proj, cache_k, cache_v, rope, *lam_args))

        merged = _merge((ya, yb, yc, yd), gates, (w_a_out, w_b_out, w_c_out, w_d_out), l)
        x, h2 = _mm_resid(merged, w_o, x, mod, l, 2, g_post_mix, (g_pre_ffn, l, 4, 3),
                          tm=512, piece=128, chunk=512, name="mm_o")
        act = _ffn_up(h2, w_ffn_up, l)
        nxt = (g_pre_mix, l + 1, 1, 0) if l + 1 < DEPTH else None
        res = _mm_resid(act, w_ffn_down, x, mod, l, 5, g_post_ffn, nxt,
                        tm=256, piece=128, chunk=512, name="mm_down")
        if nxt is not None:
            x, h = res

    y_ctx, y_lat = res
    return y_ctx.reshape(BATCH, SEQ, D_MODEL), y_lat.reshape(DEC_BATCH, DEC_SEQ, D_MODEL), new_k, new_v
```

```python
import functools
import math

import numpy as np
import jax
import jax.numpy as jnp
from jax import lax
from jax.experimental import pallas as pl
from jax.experimental.pallas import tpu as pltpu

D_MODEL = 2048
BATCH = 16
SEQ = 256
DEPTH = 4
DEC_BATCH = 2
DEC_SEQ = 1024
PAST_LEN = 512
GRID_W = 64
N_HEADS = 4
HEAD_DIM = 128
V_DIM = 2 * HEAD_DIM
ATTN_W = N_HEADS * 2 * HEAD_DIM
CONV_W = 512
CONV_K = 31
SC_W = 512
SC_K = 3
F_W = 512
F_GROUPS = 4
F_GROUP = F_W // F_GROUPS
N_BRANCH = 4
D_IN = 2 * CONV_W + 3 * ATTN_W + 3 * SC_W + F_W
D_FF = ((8 * D_MODEL + 3 * 256 - 1) // (3 * 256)) * 256
ROPE_THETA = 10000.0
EPS = 1e-6

P_ROWS = BATCH * SEQ
S_ROWS = DEC_BATCH * DEC_SEQ
N_ROWS = P_ROWS + S_ROWS
N_GROUPS = 1 + DEC_BATCH
N_MOD = 6
ADA_ROWS = 8

COL_A = 0
COL_Q = 2 * CONV_W
COL_K = COL_Q + ATTN_W
COL_V = COL_K + ATTN_W
COL_SB = COL_V + ATTN_W
COL_SC = COL_SB + SC_W
COL_SX = COL_SC + SC_W
COL_D = COL_SX + SC_W

SUBLANES = 8
CONV_PAD = 16
CONV_ROWS = 64
VMEM_LIMIT = 56 * 1024 * 1024
MM_PIECE = 256

F32 = jnp.float32
BF16 = jnp.bfloat16


def _params(n_axes, vmem=VMEM_LIMIT):
    return pltpu.CompilerParams(dimension_semantics=("arbitrary",) * n_axes, vmem_limit_bytes=vmem)


def _group_of_tile(i, tm):
    pt = P_ROWS // tm
    st = DEC_SEQ // tm
    return jnp.maximum(i - (pt - st), 0) // st


def _mod_spec(which, tm, tile_of):
    def index(*ids):
        return (_group_of_tile(tile_of(*ids), tm) * N_MOD + which, 0, 0)
    return pl.BlockSpec((None, 1, D_MODEL), index)


def _mod_vectors(ada_rows):
    return ada_rows[:N_GROUPS].reshape(N_GROUPS * N_MOD, 1, D_MODEL)


def _layer_vec_spec(width, layer):
    return pl.BlockSpec((None, 1, width), lambda *ids: (layer, 0, 0))


def _ada_chunk(c_ref, w_ref, b_ref, o_ref):
    s = _silu(c_ref[...]).astype(BF16)
    o_ref[...] = jnp.dot(s, w_ref[...].astype(BF16), preferred_element_type=F32) + b_ref[...]


def _ada(cs, w_ada, b_ada, layer):
    tn = 1024
    n = N_MOD * D_MODEL
    return pl.pallas_call(
        _ada_chunk,
        grid=(n // tn,),
        in_specs=[
            pl.BlockSpec((ADA_ROWS, D_MODEL), lambda j: (0, 0)),
            pl.BlockSpec((None, D_MODEL, tn), lambda j: (layer, 0, j)),
            pl.BlockSpec((None, 1, tn), lambda j: (layer, 0, j)),
        ],
        out_specs=pl.BlockSpec((ADA_ROWS, tn), lambda j: (0, j)),
        out_shape=jax.ShapeDtypeStruct((ADA_ROWS, n), F32),
        compiler_params=_params(1),
        name="ada",
    )(cs, w_ada, b_ada)


def _rms(x):
    return x * lax.rsqrt(jnp.mean(x * x, axis=-1, keepdims=True) + EPS)


def _sigmoid(x):
    return 0.5 * jnp.tanh(0.5 * x) + 0.5


def _silu(x):
    return x * _sigmoid(x)


def _normmod_kernel(xc_ref, xl_ref, g_ref, sc_ref, sh_ref, x_ref, h_ref, *, ctx_tiles):
    def tile(src_ref):
        x = src_ref[...]
        x_ref[...] = x
        y = _rms(x) * g_ref[...]
        h_ref[...] = (y * (1.0 + sc_ref[...]) + sh_ref[...]).astype(BF16)

    is_ctx = pl.program_id(0) < ctx_tiles
    pl.when(is_ctx)(lambda: tile(xc_ref))
    pl.when(jnp.logical_not(is_ctx))(lambda: tile(xl_ref))


def _normmod(x_ctx, x_lat, g, mod, layer, which_sc, which_sh):
    tm = 256
    ctx_tiles = P_ROWS // tm
    row = pl.BlockSpec((tm, D_MODEL), lambda i: (i, 0))
    tile = lambda i: i
    return pl.pallas_call(
        functools.partial(_normmod_kernel, ctx_tiles=ctx_tiles),
        grid=(N_ROWS // tm,),
        in_specs=[pl.BlockSpec((tm, D_MODEL), lambda i: (jnp.minimum(i, ctx_tiles - 1), 0)),
                  pl.BlockSpec((tm, D_MODEL), lambda i: (jnp.maximum(i - ctx_tiles, 0), 0)),
                  _layer_vec_spec(D_MODEL, layer),
                  _mod_spec(which_sc, tm, tile), _mod_spec(which_sh, tm, tile)],
        out_specs=[row, row],
        out_shape=[jax.ShapeDtypeStruct((N_ROWS, D_MODEL), F32), jax.ShapeDtypeStruct((N_ROWS, D_MODEL), BF16)],
        compiler_params=_params(1),
        name="normmod",
    )(x_ctx, x_lat, g, mod, mod)


def _cast_weight(w_ref, wbf_ref):
    rows = w_ref.shape[0]
    chunk = 256

    def body(r, carry):
        r0 = pl.multiple_of(r * chunk, chunk)
        wbf_ref[pl.ds(r0, chunk), :] = w_ref[pl.ds(r0, chunk), :].astype(BF16)
        return carry

    lax.fori_loop(0, rows // chunk, body, 0)


def _mm_kernel(x_ref, w_ref, *rest, sigmoid_bias, with_ada):
    rest = list(rest)
    b_ref = rest.pop(0) if sigmoid_bias else None
    if with_ada:
        c_ref, wa_ref, ba_ref, o_ref, mod_ref, wbf_ref = rest
    else:
        o_ref, wbf_ref = rest

    @pl.when(pl.program_id(1) == 0)
    def _():
        _cast_weight(w_ref, wbf_ref)

    x = x_ref[...]
    for c in range(0, o_ref.shape[1], MM_PIECE):
        cols = slice(c, c + MM_PIECE)
        acc = jnp.dot(x, wbf_ref[:, cols], preferred_element_type=F32)
        if sigmoid_bias:
            acc = _sigmoid(acc + b_ref[:, cols])
        o_ref[:, cols] = acc.astype(o_ref.dtype)
        if with_ada and c == 0:
            _ada_chunk(c_ref, wa_ref, ba_ref, mod_ref)


def _mm(x, w, layer, *, tm, tn, out_dtype, bias=None, ada=None, name):
    m, k = x.shape
    n = w.shape[2]
    steps = (n // tn) * (m // tm)
    in_specs = [pl.BlockSpec((tm, k), lambda j, i: (i, 0)),
                pl.BlockSpec((None, k, tn), lambda j, i: (layer, 0, j))]
    args = [x, w]
    out_specs = [pl.BlockSpec((tm, tn), lambda j, i: (i, j))]
    out_shape = [jax.ShapeDtypeStruct((m, n), out_dtype)]
    if bias is not None:
        in_specs.append(pl.BlockSpec((None, 1, tn), lambda j, i: (layer, 0, j)))
        args.append(bias)
    if ada is not None:
        cs, w_ada, b_ada, layer_a = ada
        n_ada = w_ada.shape[2]
        ta = n_ada // steps
        assert ta * steps == n_ada and ta % 128 == 0
        step = lambda j, i: j * (m // tm) + i
        in_specs += [pl.BlockSpec((ADA_ROWS, D_MODEL), lambda j, i: (0, 0)),
                     pl.BlockSpec((None, D_MODEL, ta), lambda j, i: (layer_a, 0, step(j, i))),
                     pl.BlockSpec((None, 1, ta), lambda j, i: (layer_a, 0, step(j, i)))]
        args += [cs, w_ada, b_ada]
        out_specs.append(pl.BlockSpec((ADA_ROWS, ta), lambda j, i: (0, step(j, i))))
        out_shape.append(jax.ShapeDtypeStruct((ADA_ROWS, n_ada), F32))
    res = pl.pallas_call(
        functools.partial(_mm_kernel, sigmoid_bias=bias is not None, with_ada=ada is not None),
        grid=(n // tn, m // tm),
        in_specs=in_specs,
        out_specs=out_specs,
        out_shape=out_shape,
        scratch_shapes=[pltpu.VMEM((k, tn), BF16)],
        compiler_params=_params(2),
        name=name,
    )(*args)
    return res if ada is not None else res[0]


def _merge_kernel(*refs, ctx_tiles):
    ctx_branches, lat_branches = refs[0:N_BRANCH], refs[N_BRANCH:2 * N_BRANCH]
    gate_refs = refs[2 * N_BRANCH:3 * N_BRANCH]
    w_refs = refs[3 * N_BRANCH:4 * N_BRANCH]
    o_ref = refs[4 * N_BRANCH]
    w_scratch = refs[4 * N_BRANCH + 1:]

    @pl.when(pl.program_id(1) == 0)
    def _():
        for w_ref, w_s in zip(w_refs, w_scratch):
            _cast_weight(w_ref, w_s)

    def merge(branches):
        acc = None
        for y_ref, g_ref, w_s in zip(branches, gate_refs, w_scratch):
            term = g_ref[...].astype(F32) * jnp.dot(y_ref[...], w_s[...], preferred_element_type=F32)
            acc = term if acc is None else acc + term
        o_ref[...] = acc.astype(BF16)

    is_ctx = pl.program_id(1) < ctx_tiles
    pl.when(is_ctx)(lambda: merge(ctx_branches))
    pl.when(jnp.logical_not(is_ctx))(lambda: merge(lat_branches))


def _merge(branches, gates, weights, layer):
    tm, tn = 512, 1024
    nt = D_MODEL // tn
    ctx_tiles = P_ROWS // tm
    widths = [w.shape[1] for w in weights]

    def ctx(width):
        return pl.BlockSpec((tm, width), lambda j, i: (jnp.minimum(i, ctx_tiles - 1), 0))

    def lat(width):
        return pl.BlockSpec((tm, width), lambda j, i: (jnp.maximum(i - ctx_tiles, 0), 0))

    def gate(branch):
        return pl.BlockSpec((tm, tn), lambda j, i: (i, branch * nt + j))

    def wt(width):
        return pl.BlockSpec((None, width, tn), lambda j, i: (layer, 0, j))

    return pl.pallas_call(
        functools.partial(_merge_kernel, ctx_tiles=ctx_tiles),
        grid=(nt, N_ROWS // tm),
        in_specs=([ctx(w) for w in widths] + [lat(w) for w in widths]
                  + [gate(b) for b in range(N_BRANCH)] + [wt(w) for w in widths]),
        out_specs=pl.BlockSpec((tm, tn), lambda j, i: (i, j)),
        out_shape=jax.ShapeDtypeStruct((N_ROWS, D_MODEL), BF16),
        scratch_shapes=[pltpu.VMEM((w, tn), BF16) for w in widths],
        compiler_params=_params(2),
        name="merge",
    )(*[b[0] for b in branches], *[b[1] for b in branches], *([gates] * N_BRANCH), *weights)


def _ffn_up_kernel(x_ref, wu_ref, wg_ref, o_ref, wu_s, wg_s):
    @pl.when(pl.program_id(1) == 0)
    def _():
        _cast_weight(wu_ref, wu_s)
        _cast_weight(wg_ref, wg_s)

    x = x_ref[...]
    for c in range(0, o_ref.shape[1], MM_PIECE):
        cols = slice(c, c + MM_PIECE)
        up = jnp.dot(x, wu_s[:, cols], preferred_element_type=F32)
        gt = jnp.dot(x, wg_s[:, cols], preferred_element_type=F32)
        o_ref[:, cols] = (_silu(gt) * up).astype(BF16)


def _ffn_up(h, w_up, layer):
    tm, tn = 2048, 512
    nt = D_FF // tn
    return pl.pallas_call(
        _ffn_up_kernel,
        grid=(nt, N_ROWS // tm),
        in_specs=[pl.BlockSpec((tm, D_MODEL), lambda j, i: (i, 0)),
                  pl.BlockSpec((None, D_MODEL, tn), lambda j, i: (layer, 0, j)),
                  pl.BlockSpec((None, D_MODEL, tn), lambda j, i: (layer, 0, nt + j))],
        out_specs=pl.BlockSpec((tm, tn), lambda j, i: (i, j)),
        out_shape=jax.ShapeDtypeStruct((N_ROWS, D_FF), BF16),
        scratch_shapes=[pltpu.VMEM((D_MODEL, tn), BF16), pltpu.VMEM((D_MODEL, tn), BF16)],
        compiler_params=_params(2),
        name="ffn_up",
    )(h, w_up, w_up)


EPI_ROWS = 16


def _mm_resid_kernel(*refs, nw, chunk, tm, piece, with_next):
    if with_next:
        a_ref, w_ref, x_ref, gate_ref, gpost_ref, gpre_ref, sc_ref, sh_ref, xo_ref, h_ref, wbf_ref = refs
    else:
        a_ref, w_ref, x_ref, gate_ref, gpost_ref, yctx_ref, ylat_ref, wbf_ref = refs
    s = pl.program_id(0)

    @pl.when(s < nw)
    def _():
        r0 = pl.multiple_of(s * chunk, chunk)
        wbf_ref[pl.ds(r0, chunk), :] = w_ref[...].astype(BF16)

    def row_tile(xo_ref):
        gg = gate_ref[...] * gpost_ref[...]
        if with_next:
            pp = gpre_ref[...] * (1.0 + sc_ref[...])
            sh = sh_ref[...]
        for p0 in range(0, tm, piece):
            f = jnp.dot(a_ref[p0:p0 + piece, :], wbf_ref[...], preferred_element_type=F32)
            for r in range(0, piece, EPI_ROWS):
                rows = slice(p0 + r, p0 + r + EPI_ROWS)
                xn = x_ref[rows, :] + _rms(f[r:r + EPI_ROWS, :]) * gg
                xo_ref[rows, :] = xn
                if with_next:
                    h_ref[rows, :] = (_rms(xn) * pp + sh).astype(BF16)

    if with_next:
        pl.when(s >= nw)(lambda: row_tile(xo_ref))
    else:
        first_lat = nw + P_ROWS // tm
        pl.when((s >= nw) & (s < first_lat))(lambda: row_tile(yctx_ref))
        pl.when(s >= first_lat)(lambda: row_tile(ylat_ref))


def _mm_resid(a, w, x, mod, layer, which_gate, g_post, nxt, *, tm, piece, chunk, name):
    k = a.shape[1]
    nw = k // chunk
    nt = N_ROWS // tm
    ctx_tiles = P_ROWS // tm
    tile = lambda s: jnp.maximum(s - nw, 0)
    row = pl.BlockSpec((tm, D_MODEL), lambda s: (tile(s), 0))
    in_specs = [pl.BlockSpec((tm, k), lambda s: (tile(s), 0)),
                pl.BlockSpec((None, chunk, D_MODEL), lambda s: (layer, jnp.minimum(s, nw - 1), 0)),
                row, _mod_spec(which_gate, tm, tile), _layer_vec_spec(D_MODEL, layer)]
    args = [a, w, x, mod, g_post]
    if nxt is not None:
        g_pre, nl, wsc, wsh, mod_next = nxt
        in_specs += [_layer_vec_spec(D_MODEL, nl), _mod_spec(wsc, tm, tile), _mod_spec(wsh, tm, tile)]
        args += [g_pre, mod_next, mod_next]
        out_specs = [row, row]
        out_shape = [jax.ShapeDtypeStruct((N_ROWS, D_MODEL), F32), jax.ShapeDtypeStruct((N_ROWS, D_MODEL), BF16)]
    else:
        out_specs = [pl.BlockSpec((tm, D_MODEL), lambda s: (jnp.minimum(tile(s), ctx_tiles - 1), 0)),
                     pl.BlockSpec((tm, D_MODEL), lambda s: (jnp.maximum(tile(s) - ctx_tiles, 0), 0))]
        out_shape = [jax.ShapeDtypeStruct((P_ROWS, D_MODEL), F32), jax.ShapeDtypeStruct((S_ROWS, D_MODEL), F32)]
    return pl.pallas_call(
        functools.partial(_mm_resid_kernel, nw=nw, chunk=chunk, tm=tm, piece=piece, with_next=nxt is not None),
        grid=(nw + nt,),
        in_specs=in_specs,
        out_specs=out_specs,
        out_shape=out_shape,
        scratch_shapes=[pltpu.VMEM((k, D_MODEL), BF16)],
        compiler_params=_params(1),
        name=name,
    )(*args)


SEQ_BLOCK = DEC_SEQ


def _seq_spec(rows, width, col, row0):
    return pl.BlockSpec((rows, width), lambda b: (row0 // rows + b, col // width))


def _seq_call(kernel_fn, name, t, nseq, in_specs, args, width, scratch):
    return pl.pallas_call(
        kernel_fn,
        grid=(nseq * t // SEQ_BLOCK,),
        in_specs=in_specs,
        out_specs=pl.BlockSpec((SEQ_BLOCK, width), lambda b: (b, 0)),
        out_shape=jax.ShapeDtypeStruct((nseq * t, width), BF16),
        scratch_shapes=scratch,
        compiler_params=_params(1),
        name=name,
    )(*args)


def _dwconv_chunk(pad_ref, w_ref, r0, ksize):
    win = CONV_ROWS + 2 * CONV_PAD
    window = pad_ref[r0:r0 + win, :]
    acc = None
    for b in range(SUBLANES):
        taps = [j for j in range(ksize) if (CONV_PAD - ksize // 2 + j) % SUBLANES == b]
        if not taps:
            continue
        shifted = window if b == 0 else pltpu.roll(window, win - b, 0)
        for j in taps:
            a0 = (CONV_PAD - ksize // 2 + j) - b
            term = w_ref[j:j + 1, :] * shifted[a0:a0 + CONV_ROWS, :]
            acc = term if acc is None else acc + term
    return acc


def _fill_pad(pad_ref, u, t):
    zeros = jnp.zeros((CONV_PAD, pad_ref.shape[1]), F32)
    pad_ref[0:CONV_PAD, :] = zeros
    pad_ref[CONV_PAD + t:2 * CONV_PAD + t, :] = zeros
    pad_ref[CONV_PAD:CONV_PAD + t, :] = u


def _conv_a_kernel(a_ref, w_ref, cb_ref, lg_ref, lb_ref, o_ref, pad_ref, *, t):
    for base in range(0, SEQ_BLOCK, t):
        seq = slice(base, base + t)
        _fill_pad(pad_ref, a_ref[seq, 0:CONV_W] * _sigmoid(a_ref[seq, CONV_W:2 * CONV_W]), t)
        for r0 in range(0, t, CONV_ROWS):
            acc = _dwconv_chunk(pad_ref, w_ref, r0, CONV_K) + cb_ref[...]
            mu = jnp.mean(acc, axis=-1, keepdims=True)
            xc = acc - mu
            y = xc * lax.rsqrt(jnp.mean(xc * xc, axis=-1, keepdims=True) + EPS)
            y = y * lg_ref[...] + lb_ref[...]
            o_ref[base + r0:base + r0 + CONV_ROWS, :] = _silu(y).astype(BF16)


def _conv_a(proj, conv_w, conv_b, ln_g, ln_b, layer, t, nseq, row0):
    in_specs = [_seq_spec(SEQ_BLOCK, 2 * CONV_W, COL_A, row0),
                pl.BlockSpec((None, CONV_K, CONV_W), lambda b: (layer, 0, 0)),
                _layer_vec_spec(CONV_W, layer), _layer_vec_spec(CONV_W, layer),
                _layer_vec_spec(CONV_W, layer)]
    return _seq_call(functools.partial(_conv_a_kernel, t=t), "conv_a", t, nseq, in_specs,
                     [proj, conv_w, conv_b, ln_g, ln_b], CONV_W,
                     [pltpu.VMEM((t + 2 * CONV_PAD, CONV_W), F32)])


def _conv_c_kernel(sb_ref, sc_ref, sx_ref, w_ref, o_ref, pad_ref, *, t):
    for base in range(0, SEQ_BLOCK, t):
        seq = slice(base, base + t)
        _fill_pad(pad_ref, sc_ref[seq, :] * sx_ref[seq, :], t)
        for r0 in range(0, t, CONV_ROWS):
            rows = slice(base + r0, base + r0 + CONV_ROWS)
            o_ref[rows, :] = (sb_ref[rows, :] * _dwconv_chunk(pad_ref, w_ref, r0, SC_K)).astype(BF16)


def _conv_c(proj, conv_w, layer, t, nseq, row0):
    in_specs = [_seq_spec(SEQ_BLOCK, SC_W, COL_SB, row0), _seq_spec(SEQ_BLOCK, SC_W, COL_SC, row0),
                _seq_spec(SEQ_BLOCK, SC_W, COL_SX, row0),
                pl.BlockSpec((None, SC_K, SC_W), lambda b: (layer, 0, 0))]
    return _seq_call(functools.partial(_conv_c_kernel, t=t), "conv_c", t, nseq, in_specs,
                     [proj, proj, proj, conv_w], SC_W,
                     [pltpu.VMEM((t + 2 * CONV_PAD, SC_W), F32)])


def _fourier_kernel(u_ref, cc_ref, sc_ref, ct_ref, st_ref, o_ref):
    t = ct_ref.shape[0]
    for base in range(0, SEQ_BLOCK, t):
        seq = slice(base, base + t)
        u = u_ref[seq, :].astype(BF16)
        yc = jnp.dot(u, cc_ref[...], preferred_element_type=F32).astype(BF16)
        ys = jnp.dot(u, sc_ref[...], preferred_element_type=F32).astype(BF16)
        f = (jnp.dot(ct_ref[...], yc, preferred_element_type=F32)
             - jnp.dot(st_ref[...], ys, preferred_element_type=F32))
        o_ref[seq, :] = f.astype(BF16)


def _dft_tables(n):
    idx = np.arange(n)
    ang = 2.0 * np.pi * ((idx[:, None] * idx[None, :]) % n) / n
    s = 1.0 / math.sqrt(n)
    return (np.cos(ang) * s).astype(np.float32), (np.sin(ang) * s).astype(np.float32)


def _fourier_tables():
    cg, sg = _dft_tables(F_GROUP)
    eye = np.eye(F_GROUPS, dtype=np.float32)
    tabs = {"chan": (np.kron(eye, cg), np.kron(eye, sg)), SEQ: _dft_tables(SEQ), DEC_SEQ: _dft_tables(DEC_SEQ)}
    return {k: tuple(jnp.asarray(m).astype(BF16) for m in v) for k, v in tabs.items()}


def _fourier(proj, tabs, t, nseq, row0):
    full = lambda n: pl.BlockSpec((n, n), lambda b: (0, 0))
    in_specs = [_seq_spec(SEQ_BLOCK, F_W, COL_D, row0), full(F_W), full(F_W), full(t), full(t)]
    return _seq_call(_fourier_kernel, "fourier", t, nseq, in_specs,
                     [proj, *tabs["chan"], *tabs[t]], F_W, [])


def _lam_of(lq1_ref, lk1_ref, lq2_ref, lk2_ref, lam_init):
    s1 = jnp.sum(lq1_ref[...] * lk1_ref[...], axis=-1, keepdims=True)
    s2 = jnp.sum(lq2_ref[...] * lk2_ref[...], axis=-1, keepdims=True)
    return jnp.exp(s1) - jnp.exp(s2) + lam_init


LOGIT_SCALE = HEAD_DIM ** -0.5 * math.log2(math.e)


def _exp2_and_rsum(t):
    e = jnp.exp2(t - jnp.max(t, axis=-1, keepdims=True))
    return e, 1.0 / jnp.sum(e, axis=-1, keepdims=True)


def _transposed_chunks(k):
    return [k[:, c * HEAD_DIM:(c + 1) * HEAD_DIM].T.astype(BF16) for c in range(ATTN_W // HEAD_DIM)]


def _diff_attn_heads(q, kt_of, v_of, lam, subln, lam_init, o_ref):
    def logits(h):
        maps = []
        for m in range(2):
            c = 2 * h + m
            qc = (q[:, c * HEAD_DIM:(c + 1) * HEAD_DIM] * LOGIT_SCALE).astype(BF16)
            maps.append(jnp.dot(qc, kt_of(c), preferred_element_type=F32))
        return maps

    pending = logits(0)
    for h in range(N_HEADS):
        following = logits(h + 1) if h + 1 < N_HEADS else None
        (e1, r1), (e2, r2) = (_exp2_and_rsum(t) for t in pending)
        a = (e1 * r1 - e2 * (lam * r2)).astype(BF16)
        o = jnp.dot(a, v_of(h), preferred_element_type=F32)
        o = _rms(o) * subln * (1.0 - lam_init)
        o_ref[:, h * V_DIM:(h + 1) * V_DIM] = o.astype(BF16)
        pending = following


def _attn_ctx_kernel(q_ref, k_ref, v_ref, lq1, lk1, lq2, lk2, sg_ref, nk_in, nv_in, o_ref, nk_ref, nv_ref, *,
                     lam_init):
    lam = _lam_of(lq1, lk1, lq2, lk2, lam_init)
    k = k_ref[...]
    v = v_ref[...]
    for h in range(N_HEADS):
        nk_ref[:, h, :] = k[:, h * V_DIM:(h + 1) * V_DIM]
        nv_ref[:, h, :] = v[:, h * V_DIM:(h + 1) * V_DIM]
    kt =_transposed_chunks(k)
    vb = v.astype(BF16)
    _diff_attn_heads(q_ref[...],
                     lambda c: kt[c],
                     lambda h: vb[:, h * V_DIM:(h + 1) * V_DIM],
                     lam, sg_ref[...], lam_init, o_ref)


def _attn_ctx(proj, lam_q1, lam_k1, lam_q2, lam_k2, subln_g, layer, new_k, new_v):
    lam_init = 0.8 - 0.6 * math.exp(-0.3 * layer)
    lv = _layer_vec_spec(HEAD_DIM, layer)
    hbm = pl.BlockSpec(memory_space=pl.ANY)
    in_specs = [_seq_spec(SEQ, ATTN_W, COL_Q, 0), _seq_spec(SEQ, ATTN_W, COL_K, 0),
                _seq_spec(SEQ, ATTN_W, COL_V, 0), lv, lv, lv, lv, _layer_vec_spec(V_DIM, layer), hbm, hbm]
    args = [proj, proj, proj, lam_q1, lam_k1, lam_q2, lam_k2, subln_g, new_k, new_v]
    kv_spec = pl.BlockSpec((None, None, SEQ, N_HEADS, V_DIM), lambda b: (b, layer, 0, 0, 0))
    kv_shape = jax.ShapeDtypeStruct(new_k.shape, F32)
    return pl.pallas_call(
        functools.partial(_attn_ctx_kernel, lam_init=lam_init),
        grid=(BATCH,),
        in_specs=in_specs,
        out_specs=[pl.BlockSpec((SEQ, ATTN_W), lambda b: (b, 0)), kv_spec, kv_spec],
        out_shape=[jax.ShapeDtypeStruct((P_ROWS, ATTN_W), BF16), kv_shape, kv_shape],
        input_output_aliases={len(args) - 2: 1, len(args) - 1: 2},
        compiler_params=_params(1),
        name="attn_ctx",
    )(*args)


def _rope(x, cos, sin_a, sin_b):
    outs = []
    for c in range(ATTN_W // HEAD_DIM):
        xc = x[:, c * HEAD_DIM:(c + 1) * HEAD_DIM]
        fwd = pltpu.roll(xc, HEAD_DIM - HEAD_DIM // 4, 1)
        bwd = pltpu.roll(xc, HEAD_DIM // 4, 1)
        outs.append(xc * cos + fwd * sin_a + bwd * sin_b)
    return outs


def _attn_lat_kernel(q_ref, k_ref, v_ref, ck_ref, cv_ref, cosq_ref, sinaq_ref, sinbq_ref,
                     cos_ref, sina_ref, sinb_ref, lq1, lk1, lq2, lk2, sg_ref, o_ref,
                     kt_ref, vall_ref, *, lam_init):
    @pl.when(pl.program_id(1) == 0)
    def _():
        vall_ref[0:PAST_LEN, :] = cv_ref[...].astype(BF16)
        vall_ref[PAST_LEN:PAST_LEN + DEC_SEQ, :] = v_ref[...].astype(BF16)
        for c, kc in enumerate(_transposed_chunks(ck_ref[...])):
            kt_ref[c * HEAD_DIM:(c + 1) * HEAD_DIM, 0:PAST_LEN] = kc
        for c in range(ATTN_W // HEAD_DIM):
            kt = k_ref[:, c * HEAD_DIM:(c + 1) * HEAD_DIM].T
            fwd = pltpu.roll(kt, HEAD_DIM - HEAD_DIM // 4, 0)
            bwd = pltpu.roll(kt, HEAD_DIM // 4, 0)
            kr = kt * cos_ref[...] + fwd * sina_ref[...] + bwd * sinb_ref[...]
            kt_ref[c * HEAD_DIM:(c + 1) * HEAD_DIM, PAST_LEN:PAST_LEN + DEC_SEQ] = kr.astype(BF16)

    lam = _lam_of(lq1, lk1, lq2, lk2, lam_init)
    q = jnp.concatenate(_rope(q_ref[...], cosq_ref[...], sinaq_ref[...], sinbq_ref[...]), axis=-1)
    _diff_attn_heads(q,
                     lambda c: kt_ref[c * HEAD_DIM:(c + 1) * HEAD_DIM, :],
                     lambda h: vall_ref[:, h * V_DIM:(h + 1) * V_DIM],
                     lam, sg_ref[...], lam_init, o_ref)


def _rope_tables():
    t = jnp.arange(DEC_SEQ)
    row = (t // GRID_W).astype(F32)
    col = (t % GRID_W).astype(F32)
    half = HEAD_DIM // 2
    inv = ROPE_THETA ** (-jnp.arange(0, half, 2, dtype=F32) / half)
    ar = row[:, None] * inv
    ac = col[:, None] * inv
    ang = jnp.concatenate([ar, ar, ac, ac], axis=-1)
    cos, sin = jnp.cos(ang), jnp.sin(ang)
    first = (np.arange(HEAD_DIM) % half) < (half // 2)
    sin_a = jnp.where(first, -sin, 0.0)
    sin_b = jnp.where(first, 0.0, sin)
    return cos, sin_a, sin_b, cos.T, sin_a.T, sin_b.T


def _attn_lat(proj, cache_k, cache_v, rope, lam_q1, lam_k1, lam_q2, lam_k2, subln_g, layer):
    lam_init = 0.8 - 0.6 * math.exp(-0.3 * layer)
    tq = 256
    nq = DEC_SEQ // tq
    lv = _layer_vec_spec(HEAD_DIM, layer)
    qtab = pl.BlockSpec((tq, HEAD_DIM), lambda b, i: (i, 0))
    ktab = pl.BlockSpec((HEAD_DIM, DEC_SEQ), lambda b, i: (0, 0))
    cache = pl.BlockSpec((None, None, PAST_LEN, ATTN_W), lambda b, i: (b, layer, 0, 0))

    def kv(col):
        return pl.BlockSpec((DEC_SEQ, ATTN_W), lambda b, i: (P_ROWS // DEC_SEQ + b, col // ATTN_W))

    args = [proj, proj, proj, cache_k, cache_v, *rope, lam_q1, lam_k1, lam_q2, lam_k2, subln_g]
    return pl.pallas_call(
        functools.partial(_attn_lat_kernel, lam_init=lam_init),
        grid=(DEC_BATCH, nq),
        in_specs=[pl.BlockSpec((tq, ATTN_W), lambda b, i: (P_ROWS // tq + b * nq + i, COL_Q // ATTN_W)),
                  kv(COL_K), kv(COL_V), cache, cache, qtab, qtab, qtab, ktab, ktab, ktab,
                  lv, lv, lv, lv, _layer_vec_spec(V_DIM, layer)],
        out_specs=pl.BlockSpec((tq, ATTN_W), lambda b, i: (b * nq + i, 0)),
        out_shape=jax.ShapeDtypeStruct((S_ROWS, ATTN_W), BF16),
        scratch_shapes=[pltpu.VMEM((ATTN_W, PAST_LEN + DEC_SEQ), BF16),
                        pltpu.VMEM((PAST_LEN + DEC_SEQ, ATTN_W), BF16)],
        compiler_params=_params(2),
        name="attn_lat",
    )(*args)


def kernel(x_prompt, x_sample, cache_k, cache_v, c, c_ctx, w_ada, b_ada, g_pre_mix, g_post_mix,
           g_pre_ffn, g_post_ffn, w_in, conv_a_w, conv_a_b, ln_a_g, ln_a_b, w_a_out,
           lam_q1, lam_k1, lam_q2, lam_k2, subln_g, w_b_out, conv_c_w, w_c_out, w_d_out,
           w_gate, b_gate, w_o, w_ffn_up, w_ffn_down):
    cs = jnp.concatenate([c_ctx[None, :], c, jnp.zeros((ADA_ROWS - N_GROUPS, D_MODEL), F32)], axis=0)

    vec = lambda a: a.reshape(DEPTH, 1, a.shape[-1])
    g_pre_mix, g_post_mix, g_pre_ffn, g_post_ffn = map(vec, (g_pre_mix, g_post_mix, g_pre_ffn, g_post_ffn))
    conv_a_b, ln_a_g, ln_a_b, subln_g, b_gate, b_ada = map(vec, (conv_a_b, ln_a_g, ln_a_b, subln_g, b_gate, b_ada))
    mod = _mod_vectors(_ada(cs, w_ada, b_ada, 0))
    lam_q1, lam_k1, lam_q2, lam_k2 = map(vec, (lam_q1, lam_k1, lam_q2, lam_k2))
    cache_k = cache_k.reshape(DEC_BATCH, DEPTH, PAST_LEN, ATTN_W)
    cache_v = cache_v.reshape(DEC_BATCH, DEPTH, PAST_LEN, ATTN_W)
    rope = _rope_tables()
    dft = _fourier_tables()

    def both(fn, *args):
        return fn(*args, SEQ, BATCH, 0), fn(*args, DEC_SEQ, DEC_BATCH, P_ROWS)

    x, h = _normmod(x_prompt.reshape(P_ROWS, D_MODEL), x_sample.reshape(S_ROWS, D_MODEL),
                    g_pre_mix, mod, 0, 1, 0)
    new_k = jnp.zeros((BATCH, DEPTH, SEQ, N_HEADS, V_DIM), F32)
    new_v = jnp.zeros((BATCH, DEPTH, SEQ, N_HEADS, V_DIM), F32)
    for l in range(DEPTH):
        proj = _mm(h, w_in, l, tm=1024, tn=1024, out_dtype=F32, name="mm_in")
        if l + 1 < DEPTH:
            gates, ada_next = _mm(h, w_gate, l, tm=1024, tn=1024, out_dtype=BF16, bias=b_gate,
                                  ada=(cs, w_ada, b_ada, l + 1), name="mm_gate")
            mod_next = _mod_vectors(ada_next)
        else:
            gates = _mm(h, w_gate, l, tm=1024, tn=1024, out_dtype=BF16, bias=b_gate, name="mm_gate")

        ya = both(_conv_a, proj, conv_a_w, conv_a_b, ln_a_g, ln_a_b, l)
        yc = both(_conv_c, proj, conv_c_w, l)
        yd = both(_fourier, proj, dft)
        lam_args = (lam_q1, lam_k1, lam_q2, lam_k2, subln_g, l)
        yb_ctx, new_k, new_v = _attn_ctx(proj, *lam_args, new_k, new_v)
        yb = (yb_ctx, _attn_lat(proj, cache_k, cache_v, rope, *lam_args))

        merged = _merge((ya, yb, yc, yd), gates, (w_a_out, w_b_out, w_c_out, w_d_out), l)
        x, h2 = _mm_resid(merged, w_o, x, mod, l, 2, g_post_mix, (g_pre_ffn, l, 4, 3, mod),
                          tm=512, piece=128, chunk=512, name="mm_o")
        act = _ffn_up(h2, w_ffn_up, l)
        nxt = (g_pre_mix, l + 1, 1, 0, mod_next) if l + 1 < DEPTH else None
        res = _mm_resid(act, w_ffn_down, x, mod, l, 5, g_post_ffn, nxt,
                        tm=256, piece=128, chunk=512, name="mm_down")
        if nxt is not None:
            x, h = res
            mod = mod_next

    y_ctx, y_lat = res
    return y_ctx.reshape(BATCH, SEQ, D_MODEL), y_lat.reshape(DEC_BATCH, DEC_SEQ, D_MODEL), new_k, new_v
```

```python
import functools
import math

import numpy as np
import jax
import jax.numpy as jnp
from jax import lax
from jax.experimental import pallas as pl
from jax.experimental.pallas import tpu as pltpu

D_MODEL = 2048
BATCH = 16
SEQ = 256
DEPTH = 4
DEC_BATCH = 2
DEC_SEQ = 1024
PAST_LEN = 512
GRID_W = 64
N_HEADS = 4
HEAD_DIM = 128
V_DIM = 2 * HEAD_DIM
ATTN_W = N_HEADS * 2 * HEAD_DIM
CONV_W = 512
CONV_K = 31
SC_W = 512
SC_K = 3
F_W = 512
F_GROUPS = 4
F_GROUP = F_W // F_GROUPS
N_BRANCH = 4
D_IN = 2 * CONV_W + 3 * ATTN_W + 3 * SC_W + F_W
D_FF = ((8 * D_MODEL + 3 * 256 - 1) // (3 * 256)) * 256
ROPE_THETA = 10000.0
EPS = 1e-6

P_ROWS = BATCH * SEQ
S_ROWS = DEC_BATCH * DEC_SEQ
N_ROWS = P_ROWS + S_ROWS
N_GROUPS = 1 + DEC_BATCH
N_MOD = 6
ADA_ROWS = 8

IN_TILE = 1024
KV_TILES = (2, 3)
REST_TILES = (0, 1, 4, 5)
KV_K = 0
KV_V = ATTN_W
COL_A = 0
COL_Q = 2 * CONV_W
COL_SB = COL_Q + ATTN_W
COL_SC = COL_SB + SC_W
COL_SX = COL_SC + SC_W
COL_D = COL_SX + SC_W

SUBLANES = 8
CONV_PAD = 16
CONV_ROWS = 64
VMEM_LIMIT = 56 * 1024 * 1024
MM_PIECE = 256

F32 = jnp.float32
BF16 = jnp.bfloat16


def _params(n_axes, vmem=VMEM_LIMIT):
    return pltpu.CompilerParams(dimension_semantics=("arbitrary",) * n_axes, vmem_limit_bytes=vmem)


def _group_of_tile(i, tm):
    pt = P_ROWS // tm
    st = DEC_SEQ // tm
    return jnp.maximum(i - (pt - st), 0) // st


def _mod_spec(which, tm, tile_of):
    def index(*ids):
        return (_group_of_tile(tile_of(*ids), tm) * N_MOD + which, 0, 0)
    return pl.BlockSpec((None, 1, D_MODEL), index)


def _mod_vectors(ada_rows):
    return ada_rows[:N_GROUPS].reshape(N_GROUPS * N_MOD, 1, D_MODEL)


def _layer_vec_spec(width, layer):
    return pl.BlockSpec((None, 1, width), lambda *ids: (layer, 0, 0))


def _ada_chunk(c_ref, w_ref, b_ref, o_ref):
    s = _silu(c_ref[...]).astype(BF16)
    o_ref[...] = jnp.dot(s, w_ref[...].astype(BF16), preferred_element_type=F32) + b_ref[...]


def _ada(cs, w_ada, b_ada, layer):
    tn = 1024
    n = N_MOD * D_MODEL
    return pl.pallas_call(
        _ada_chunk,
        grid=(n // tn,),
        in_specs=[
            pl.BlockSpec((ADA_ROWS, D_MODEL), lambda j: (0, 0)),
            pl.BlockSpec((None, D_MODEL, tn), lambda j: (layer, 0, j)),
            pl.BlockSpec((None, 1, tn), lambda j: (layer, 0, j)),
        ],
        out_specs=pl.BlockSpec((ADA_ROWS, tn), lambda j: (0, j)),
        out_shape=jax.ShapeDtypeStruct((ADA_ROWS, n), F32),
        compiler_params=_params(1),
        name="ada",
    )(cs, w_ada, b_ada)


def _rms(x):
    return x * lax.rsqrt(jnp.mean(x * x, axis=-1, keepdims=True) + EPS)


def _sigmoid(x):
    return 0.5 * jnp.tanh(0.5 * x) + 0.5


def _silu(x):
    return x * _sigmoid(x)


def _normmod_kernel(xc_ref, xl_ref, g_ref, sc_ref, sh_ref, x_ref, h_ref, *, ctx_tiles):
    def tile(src_ref):
        x = src_ref[...]
        x_ref[...] = x
        y = _rms(x) * g_ref[...]
        h_ref[...] = (y * (1.0 + sc_ref[...]) + sh_ref[...]).astype(BF16)

    is_ctx = pl.program_id(0) < ctx_tiles
    pl.when(is_ctx)(lambda: tile(xc_ref))
    pl.when(jnp.logical_not(is_ctx))(lambda: tile(xl_ref))


def _normmod(x_ctx, x_lat, g, mod, layer, which_sc, which_sh):
    tm = 256
    ctx_tiles = P_ROWS // tm
    row = pl.BlockSpec((tm, D_MODEL), lambda i: (i, 0))
    tile = lambda i: i
    return pl.pallas_call(
        functools.partial(_normmod_kernel, ctx_tiles=ctx_tiles),
        grid=(N_ROWS // tm,),
        in_specs=[pl.BlockSpec((tm, D_MODEL), lambda i: (jnp.minimum(i, ctx_tiles - 1), 0)),
                  pl.BlockSpec((tm, D_MODEL), lambda i: (jnp.maximum(i - ctx_tiles, 0), 0)),
                  _layer_vec_spec(D_MODEL, layer),
                  _mod_spec(which_sc, tm, tile), _mod_spec(which_sh, tm, tile)],
        out_specs=[row, row],
        out_shape=[jax.ShapeDtypeStruct((N_ROWS, D_MODEL), F32), jax.ShapeDtypeStruct((N_ROWS, D_MODEL), BF16)],
        compiler_params=_params(1),
        name="normmod",
    )(x_ctx, x_lat, g, mod, mod)


def _cast_weight(w_ref, wbf_ref):
    rows = w_ref.shape[0]
    chunk = 256

    def body(r, carry):
        r0 = pl.multiple_of(r * chunk, chunk)
        wbf_ref[pl.ds(r0, chunk), :] = w_ref[pl.ds(r0, chunk), :].astype(BF16)
        return carry

    lax.fori_loop(0, rows // chunk, body, 0)


def _mm_kernel(x_ref, w_ref, *rest, sigmoid_bias, with_ada):
    rest = list(rest)
    b_ref = rest.pop(0) if sigmoid_bias else None
    if with_ada:
        c_ref, wa_ref, ba_ref, o_ref, mod_ref, wbf_ref = rest
    else:
        o_ref, wbf_ref = rest

    @pl.when(pl.program_id(1) == 0)
    def _():
        _cast_weight(w_ref, wbf_ref)

    x = x_ref[...]
    for c in range(0, o_ref.shape[1], MM_PIECE):
        cols = slice(c, c + MM_PIECE)
        acc = jnp.dot(x, wbf_ref[:, cols], preferred_element_type=F32)
        if sigmoid_bias:
            acc = _sigmoid(acc + b_ref[:, cols])
        o_ref[:, cols] = acc.astype(o_ref.dtype)
        if with_ada and c == 0:
            _ada_chunk(c_ref, wa_ref, ba_ref, mod_ref)


def _mm(x, w, layer, *, tm, tn, out_dtype, col_tiles=None, bias=None, ada=None, name):
    m, k = x.shape
    if col_tiles is None:
        col_tiles = tuple(range(w.shape[2] // tn))
    n = len(col_tiles) * tn
    steps = (n // tn) * (m // tm)

    def wcol(j):
        return sum((j == idx) * tile for idx, tile in enumerate(col_tiles))

    in_specs = [pl.BlockSpec((tm, k), lambda j, i: (i, 0)),
                pl.BlockSpec((None, k, tn), lambda j, i: (layer, 0, wcol(j)))]
    args = [x, w]
    out_specs = [pl.BlockSpec((tm, tn), lambda j, i: (i, j))]
    out_shape = [jax.ShapeDtypeStruct((m, n), out_dtype)]
    if bias is not None:
        in_specs.append(pl.BlockSpec((None, 1, tn), lambda j, i: (layer, 0, wcol(j))))
        args.append(bias)
    if ada is not None:
        cs, w_ada, b_ada, layer_a = ada
        n_ada = w_ada.shape[2]
        ta = n_ada // steps
        assert ta * steps == n_ada and ta % 128 == 0
        step = lambda j, i: j * (m // tm) + i
        in_specs += [pl.BlockSpec((ADA_ROWS, D_MODEL), lambda j, i: (0, 0)),
                     pl.BlockSpec((None, D_MODEL, ta), lambda j, i: (layer_a, 0, step(j, i))),
                     pl.BlockSpec((None, 1, ta), lambda j, i: (layer_a, 0, step(j, i)))]
        args += [cs, w_ada, b_ada]
        out_specs.append(pl.BlockSpec((ADA_ROWS, ta), lambda j, i: (0, step(j, i))))
        out_shape.append(jax.ShapeDtypeStruct((ADA_ROWS, n_ada), F32))
    res = pl.pallas_call(
        functools.partial(_mm_kernel, sigmoid_bias=bias is not None, with_ada=ada is not None),
        grid=(n // tn, m // tm),
        in_specs=in_specs,
        out_specs=out_specs,
        out_shape=out_shape,
        scratch_shapes=[pltpu.VMEM((k, tn), BF16)],
        compiler_params=_params(2),
        name=name,
    )(*args)
    return res if ada is not None else res[0]


def _merge_kernel(*refs, ctx_tiles):
    ctx_branches, lat_branches = refs[0:N_BRANCH], refs[N_BRANCH:2 * N_BRANCH]
    gate_refs = refs[2 * N_BRANCH:3 * N_BRANCH]
    w_refs = refs[3 * N_BRANCH:4 * N_BRANCH]
    o_ref = refs[4 * N_BRANCH]
    w_scratch = refs[4 * N_BRANCH + 1:]

    @pl.when(pl.program_id(1) == 0)
    def _():
        for w_ref, w_s in zip(w_refs, w_scratch):
            _cast_weight(w_ref, w_s)

    def merge(branches):
        acc = None
        for y_ref, g_ref, w_s in zip(branches, gate_refs, w_scratch):
            term = g_ref[...].astype(F32) * jnp.dot(y_ref[...], w_s[...], preferred_element_type=F32)
            acc = term if acc is None else acc + term
        o_ref[...] = acc.astype(BF16)

    is_ctx = pl.program_id(1) < ctx_tiles
    pl.when(is_ctx)(lambda: merge(ctx_branches))
    pl.when(jnp.logical_not(is_ctx))(lambda: merge(lat_branches))


def _merge(branches, gates, weights, layer):
    tm, tn = 512, 1024
    nt = D_MODEL // tn
    ctx_tiles = P_ROWS // tm
    widths = [w.shape[1] for w in weights]

    def ctx(width):
        return pl.BlockSpec((tm, width), lambda j, i: (jnp.minimum(i, ctx_tiles - 1), 0))

    def lat(width):
        return pl.BlockSpec((tm, width), lambda j, i: (jnp.maximum(i - ctx_tiles, 0), 0))

    def gate(branch):
        return pl.BlockSpec((tm, tn), lambda j, i: (i, branch * nt + j))

    def wt(width):
        return pl.BlockSpec((None, width, tn), lambda j, i: (layer, 0, j))

    return pl.pallas_call(
        functools.partial(_merge_kernel, ctx_tiles=ctx_tiles),
        grid=(nt, N_ROWS // tm),
        in_specs=([ctx(w) for w in widths] + [lat(w) for w in widths]
                  + [gate(b) for b in range(N_BRANCH)] + [wt(w) for w in widths]),
        out_specs=pl.BlockSpec((tm, tn), lambda j, i: (i, j)),
        out_shape=jax.ShapeDtypeStruct((N_ROWS, D_MODEL), BF16),
        scratch_shapes=[pltpu.VMEM((w, tn), BF16) for w in widths],
        compiler_params=_params(2),
        name="merge",
    )(*[b[0] for b in branches], *[b[1] for b in branches], *([gates] * N_BRANCH), *weights)


def _ffn_up_kernel(x_ref, wu_ref, wg_ref, o_ref, wu_s, wg_s):
    @pl.when(pl.program_id(1) == 0)
    def _():
        _cast_weight(wu_ref, wu_s)
        _cast_weight(wg_ref, wg_s)

    x = x_ref[...]
    for c in range(0, o_ref.shape[1], MM_PIECE):
        cols = slice(c, c + MM_PIECE)
        up = jnp.dot(x, wu_s[:, cols], preferred_element_type=F32)
        gt = jnp.dot(x, wg_s[:, cols], preferred_element_type=F32)
        o_ref[:, cols] = (_silu(gt) * up).astype(BF16)


def _ffn_up(h, w_up, layer):
    tm, tn = 2048, 512
    nt = D_FF // tn
    return pl.pallas_call(
        _ffn_up_kernel,
        grid=(nt, N_ROWS // tm),
        in_specs=[pl.BlockSpec((tm, D_MODEL), lambda j, i: (i, 0)),
                  pl.BlockSpec((None, D_MODEL, tn), lambda j, i: (layer, 0, j)),
                  pl.BlockSpec((None, D_MODEL, tn), lambda j, i: (layer, 0, nt + j))],
        out_specs=pl.BlockSpec((tm, tn), lambda j, i: (i, j)),
        out_shape=jax.ShapeDtypeStruct((N_ROWS, D_FF), BF16),
        scratch_shapes=[pltpu.VMEM((D_MODEL, tn), BF16), pltpu.VMEM((D_MODEL, tn), BF16)],
        compiler_params=_params(2),
        name="ffn_up",
    )(h, w_up, w_up)


EPI_ROWS = 16


def _mm_resid_kernel(*refs, nw, chunk, tm, piece, with_next):
    if with_next:
        a_ref, w_ref, x_ref, gate_ref, gpost_ref, gpre_ref, sc_ref, sh_ref, xo_ref, h_ref, wbf_ref = refs
    else:
        a_ref, w_ref, x_ref, gate_ref, gpost_ref, yctx_ref, ylat_ref, wbf_ref = refs
    s = pl.program_id(0)

    @pl.when(s < nw)
    def _():
        r0 = pl.multiple_of(s * chunk, chunk)
        wbf_ref[pl.ds(r0, chunk), :] = w_ref[...].astype(BF16)

    def row_tile(xo_ref):
        gg = gate_ref[...] * gpost_ref[...]
        if with_next:
            pp = gpre_ref[...] * (1.0 + sc_ref[...])
            sh = sh_ref[...]
        for p0 in range(0, tm, piece):
            f = jnp.dot(a_ref[p0:p0 + piece, :], wbf_ref[...], preferred_element_type=F32)
            for r in range(0, piece, EPI_ROWS):
                rows = slice(p0 + r, p0 + r + EPI_ROWS)
                xn = x_ref[rows, :] + _rms(f[r:r + EPI_ROWS, :]) * gg
                xo_ref[rows, :] = xn
                if with_next:
                    h_ref[rows, :] = (_rms(xn) * pp + sh).astype(BF16)

    if with_next:
        pl.when(s >= nw)(lambda: row_tile(xo_ref))
    else:
        first_lat = nw + P_ROWS // tm
        pl.when((s >= nw) & (s < first_lat))(lambda: row_tile(yctx_ref))
        pl.when(s >= first_lat)(lambda: row_tile(ylat_ref))


def _mm_resid(a, w, x, mod, layer, which_gate, g_post, nxt, *, tm, piece, chunk, name):
    k = a.shape[1]
    nw = k // chunk
    nt = N_ROWS // tm
    ctx_tiles = P_ROWS // tm
    tile = lambda s: jnp.maximum(s - nw, 0)
    row = pl.BlockSpec((tm, D_MODEL), lambda s: (tile(s), 0))
    in_specs = [pl.BlockSpec((tm, k), lambda s: (tile(s), 0)),
                pl.BlockSpec((None, chunk, D_MODEL), lambda s: (layer, jnp.minimum(s, nw - 1), 0)),
                row, _mod_spec(which_gate, tm, tile), _layer_vec_spec(D_MODEL, layer)]
    args = [a, w, x, mod, g_post]
    if nxt is not None:
        g_pre, nl, wsc, wsh, mod_next = nxt
        in_specs += [_layer_vec_spec(D_MODEL, nl), _mod_spec(wsc, tm, tile), _mod_spec(wsh, tm, tile)]
        args += [g_pre, mod_next, mod_next]
        out_specs = [row, row]
        out_shape = [jax.ShapeDtypeStruct((N_ROWS, D_MODEL), F32), jax.ShapeDtypeStruct((N_ROWS, D_MODEL), BF16)]
    else:
        out_specs = [pl.BlockSpec((tm, D_MODEL), lambda s: (jnp.minimum(tile(s), ctx_tiles - 1), 0)),
                     pl.BlockSpec((tm, D_MODEL), lambda s: (jnp.maximum(tile(s) - ctx_tiles, 0), 0))]
        out_shape = [jax.ShapeDtypeStruct((P_ROWS, D_MODEL), F32), jax.ShapeDtypeStruct((S_ROWS, D_MODEL), F32)]
    return pl.pallas_call(
        functools.partial(_mm_resid_kernel, nw=nw, chunk=chunk, tm=tm, piece=piece, with_next=nxt is not None),
        grid=(nw + nt,),
        in_specs=in_specs,
        out_specs=out_specs,
        out_shape=out_shape,
        scratch_shapes=[pltpu.VMEM((k, D_MODEL), BF16)],
        compiler_params=_params(1),
        name=name,
    )(*args)


SEQ_BLOCK = DEC_SEQ


def _seq_spec(rows, width, col, row0):
    return pl.BlockSpec((rows, width), lambda b: (row0 // rows + b, col // width))


def _dwconv_chunk(pad_ref, w_ref, r0, ksize):
    win = CONV_ROWS + 2 * CONV_PAD
    window = pad_ref[r0:r0 + win, :]
    acc = None
    for b in range(SUBLANES):
        taps = [j for j in range(ksize) if (CONV_PAD - ksize // 2 + j) % SUBLANES == b]
        if not taps:
            continue
        shifted = window if b == 0 else pltpu.roll(window, win - b, 0)
        for j in taps:
            a0 = (CONV_PAD - ksize // 2 + j) - b
            term = w_ref[j:j + 1, :] * shifted[a0:a0 + CONV_ROWS, :]
            acc = term if acc is None else acc + term
    return acc


def _fill_pad(pad_ref, u, t):
    zeros = jnp.zeros((CONV_PAD, pad_ref.shape[1]), F32)
    pad_ref[0:CONV_PAD, :] = zeros
    pad_ref[CONV_PAD + t:2 * CONV_PAD + t, :] = zeros
    pad_ref[CONV_PAD:CONV_PAD + t, :] = u


def _conv_a_kernel(a_ref, w_ref, cb_ref, lg_ref, lb_ref, o_ref, pad_ref, *, t):
    for base in range(0, SEQ_BLOCK, t):
        seq = slice(base, base + t)
        a_val = a_ref[seq, 0:CONV_W].astype(F32)
        _fill_pad(pad_ref, a_val * _sigmoid(a_ref[seq, CONV_W:2 * CONV_W].astype(F32)), t)
        for r0 in range(0, t, CONV_ROWS):
            acc = _dwconv_chunk(pad_ref, w_ref, r0, CONV_K) + cb_ref[...]
            mu = jnp.mean(acc, axis=-1, keepdims=True)
            xc = acc - mu
            y = xc * lax.rsqrt(jnp.mean(xc * xc, axis=-1, keepdims=True) + EPS)
            y = y * lg_ref[...] + lb_ref[...]
            o_ref[base + r0:base + r0 + CONV_ROWS, :] = _silu(y).astype(BF16)


def _conv_c_kernel(sb_ref, sc_ref, sx_ref, w_ref, o_ref, pad_ref, *, t):
    for base in range(0, SEQ_BLOCK, t):
        seq = slice(base, base + t)
        _fill_pad(pad_ref, sc_ref[seq, :].astype(F32) * sx_ref[seq, :].astype(F32), t)
        for r0 in range(0, t, CONV_ROWS):
            rows = slice(base + r0, base + r0 + CONV_ROWS)
            gated = sb_ref[rows, :].astype(F32) * _dwconv_chunk(pad_ref, w_ref, r0, SC_K)
            o_ref[rows, :] = gated.astype(BF16)


def _fourier_kernel(u_ref, cc_ref, sc_ref, ct_ref, st_ref, o_ref):
    t = ct_ref.shape[0]
    for base in range(0, SEQ_BLOCK, t):
        seq = slice(base, base + t)
        u = u_ref[seq, :].astype(BF16)
        yc = jnp.dot(u, cc_ref[...], preferred_element_type=F32).astype(BF16)
        ys = jnp.dot(u, sc_ref[...], preferred_element_type=F32).astype(BF16)
        f = (jnp.dot(ct_ref[...], yc, preferred_element_type=F32)
             - jnp.dot(st_ref[...], ys, preferred_element_type=F32))
        o_ref[seq, :] = f.astype(BF16)


def _dft_tables(n):
    idx = np.arange(n)
    ang = 2.0 * np.pi * ((idx[:, None] * idx[None, :]) % n) / n
    s = 1.0 / math.sqrt(n)
    return (np.cos(ang) * s).astype(np.float32), (np.sin(ang) * s).astype(np.float32)


def _fourier_tables():
    cg, sg = _dft_tables(F_GROUP)
    eye = np.eye(F_GROUPS, dtype=np.float32)
    tabs = {"chan": (np.kron(eye, cg), np.kron(eye, sg)), SEQ: _dft_tables(SEQ), DEC_SEQ: _dft_tables(DEC_SEQ)}
    return {k: tuple(jnp.asarray(m).astype(BF16) for m in v) for k, v in tabs.items()}


def _branches_kernel(a_ref, sb_ref, sc_ref, sx_ref, d_ref, wa_ref, cb_ref, lg_ref, lb_ref, wc_ref,
                     cc_ref, cs_ref, ct_ref, st_ref, ya_ref, yc_ref, yd_ref, pad_a, pad_c, *, t):
    _conv_a_kernel(a_ref, wa_ref, cb_ref, lg_ref, lb_ref, ya_ref, pad_a, t=t)
    _conv_c_kernel(sb_ref, sc_ref, sx_ref, wc_ref, yc_ref, pad_c, t=t)
    _fourier_kernel(d_ref, cc_ref, cs_ref, ct_ref, st_ref, yd_ref)


def _branches(rest, conv_a_w, conv_a_b, ln_a_g, ln_a_b, conv_c_w, tabs, layer, t, nseq, row0):
    full = lambda n: pl.BlockSpec((n, n), lambda b: (0, 0))
    lv = _layer_vec_spec(CONV_W, layer)
    in_specs = [_seq_spec(SEQ_BLOCK, 2 * CONV_W, COL_A, row0),
                _seq_spec(SEQ_BLOCK, SC_W, COL_SB, row0), _seq_spec(SEQ_BLOCK, SC_W, COL_SC, row0),
                _seq_spec(SEQ_BLOCK, SC_W, COL_SX, row0), _seq_spec(SEQ_BLOCK, F_W, COL_D, row0),
                pl.BlockSpec((None, CONV_K, CONV_W), lambda b: (layer, 0, 0)), lv, lv, lv,
                pl.BlockSpec((None, SC_K, SC_W), lambda b: (layer, 0, 0)),
                full(F_W), full(F_W), full(t), full(t)]
    out = pl.BlockSpec((SEQ_BLOCK, CONV_W), lambda b: (b, 0))
    out_shape = jax.ShapeDtypeStruct((nseq * t, CONV_W), BF16)
    return pl.pallas_call(
        functools.partial(_branches_kernel, t=t),
        grid=(nseq * t // SEQ_BLOCK,),
        in_specs=in_specs,
        out_specs=[out, out, out],
        out_shape=[out_shape, out_shape, out_shape],
        scratch_shapes=[pltpu.VMEM((t + 2 * CONV_PAD, CONV_W), F32), pltpu.VMEM((t + 2 * CONV_PAD, SC_W), F32)],
        compiler_params=_params(1),
        name="branches",
    )(rest, rest, rest, rest, rest, conv_a_w, conv_a_b, ln_a_g, ln_a_b, conv_c_w, *tabs["chan"], *tabs[t])


def _lam_of(lq1_ref, lk1_ref, lq2_ref, lk2_ref, lam_init):
    s1 = jnp.sum(lq1_ref[...] * lk1_ref[...], axis=-1, keepdims=True)
    s2 = jnp.sum(lq2_ref[...] * lk2_ref[...], axis=-1, keepdims=True)
    return jnp.exp(s1) - jnp.exp(s2) + lam_init


LOGIT_SCALE = HEAD_DIM ** -0.5 * math.log2(math.e)


def _exp2_and_rsum(t):
    e = jnp.exp2(t - jnp.max(t, axis=-1, keepdims=True))
    return e, 1.0 / jnp.sum(e, axis=-1, keepdims=True)


def _transposed_chunks(k):
    return [k[:, c * HEAD_DIM:(c + 1) * HEAD_DIM].T.astype(BF16) for c in range(ATTN_W // HEAD_DIM)]


def _diff_attn_heads(q, kt_of, v_of, lam, subln, lam_init, o_ref):
    def logits(h):
        maps = []
        for m in range(2):
            c = 2 * h + m
            qc = (q[:, c * HEAD_DIM:(c + 1) * HEAD_DIM] * LOGIT_SCALE).astype(BF16)
            maps.append(jnp.dot(qc, kt_of(c), preferred_element_type=F32))
        return maps

    pending = logits(0)
    for h in range(N_HEADS):
        following = logits(h + 1) if h + 1 < N_HEADS else None
        (e1, r1), (e2, r2) = (_exp2_and_rsum(t) for t in pending)
        a = (e1 * r1 - e2 * (lam * r2)).astype(BF16)
        o = jnp.dot(a, v_of(h), preferred_element_type=F32)
        o = _rms(o) * subln * (1.0 - lam_init)
        o_ref[:, h * V_DIM:(h + 1) * V_DIM] = o.astype(BF16)
        pending = following


def _attn_ctx_kernel(q_ref, k_ref, v_ref, lq1, lk1, lq2, lk2, sg_ref, nk_in, nv_in, o_ref, nk_ref, nv_ref, *,
                     lam_init):
    lam = _lam_of(lq1, lk1, lq2, lk2, lam_init)
    k = k_ref[...]
    v = v_ref[...]
    for h in range(N_HEADS):
        nk_ref[:, h, :] = k[:, h * V_DIM:(h + 1) * V_DIM]
        nv_ref[:, h, :] = v[:, h * V_DIM:(h + 1) * V_DIM]
    kt =_transposed_chunks(k)
    vb = v.astype(BF16)
    _diff_attn_heads(q_ref[...].astype(F32),
                     lambda c: kt[c],
                     lambda h: vb[:, h * V_DIM:(h + 1) * V_DIM],
                     lam, sg_ref[...], lam_init, o_ref)


def _attn_ctx(rest, kv, lam_q1, lam_k1, lam_q2, lam_k2, subln_g, layer, new_k, new_v):
    lam_init = 0.8 - 0.6 * math.exp(-0.3 * layer)
    lv = _layer_vec_spec(HEAD_DIM, layer)
    hbm = pl.BlockSpec(memory_space=pl.ANY)
    in_specs = [_seq_spec(SEQ, ATTN_W, COL_Q, 0), _seq_spec(SEQ, ATTN_W, KV_K, 0),
                _seq_spec(SEQ, ATTN_W, KV_V, 0), lv, lv, lv, lv, _layer_vec_spec(V_DIM, layer), hbm, hbm]
    args = [rest, kv, kv, lam_q1, lam_k1, lam_q2, lam_k2, subln_g, new_k, new_v]
    kv_spec = pl.BlockSpec((None, None, SEQ, N_HEADS, V_DIM), lambda b: (b, layer, 0, 0, 0))
    kv_shape = jax.ShapeDtypeStruct(new_k.shape, F32)
    return pl.pallas_call(
        functools.partial(_attn_ctx_kernel, lam_init=lam_init),
        grid=(BATCH,),
        in_specs=in_specs,
        out_specs=[pl.BlockSpec((SEQ, ATTN_W), lambda b: (b, 0)), kv_spec, kv_spec],
        out_shape=[jax.ShapeDtypeStruct((P_ROWS, ATTN_W), BF16), kv_shape, kv_shape],
        input_output_aliases={len(args) - 2: 1, len(args) - 1: 2},
        compiler_params=_params(1),
        name="attn_ctx",
    )(*args)


def _rope(x, cos, sin_a, sin_b):
    outs = []
    for c in range(ATTN_W // HEAD_DIM):
        xc = x[:, c * HEAD_DIM:(c + 1) * HEAD_DIM]
        fwd = pltpu.roll(xc, HEAD_DIM - HEAD_DIM // 4, 1)
        bwd = pltpu.roll(xc, HEAD_DIM // 4, 1)
        outs.append(xc * cos + fwd * sin_a + bwd * sin_b)
    return outs


def _attn_lat_kernel(q_ref, k_ref, v_ref, ck_ref, cv_ref, cosq_ref, sinaq_ref, sinbq_ref,
                     cos_ref, sina_ref, sinb_ref, lq1, lk1, lq2, lk2, sg_ref, o_ref,
                     kt_ref, vall_ref, *, lam_init):
    @pl.when(pl.program_id(1) == 0)
    def _():
        vall_ref[0:PAST_LEN, :] = cv_ref[...].astype(BF16)
        vall_ref[PAST_LEN:PAST_LEN + DEC_SEQ, :] = v_ref[...].astype(BF16)
        for c, kc in enumerate(_transposed_chunks(ck_ref[...])):
            kt_ref[c * HEAD_DIM:(c + 1) * HEAD_DIM, 0:PAST_LEN] = kc
        for c in range(ATTN_W // HEAD_DIM):
            kt = k_ref[:, c * HEAD_DIM:(c + 1) * HEAD_DIM].T
            fwd = pltpu.roll(kt, HEAD_DIM - HEAD_DIM // 4, 0)
            bwd = pltpu.roll(kt, HEAD_DIM // 4, 0)
            kr = kt * cos_ref[...] + fwd * sina_ref[...] + bwd * sinb_ref[...]
            kt_ref[c * HEAD_DIM:(c + 1) * HEAD_DIM, PAST_LEN:PAST_LEN + DEC_SEQ] = kr.astype(BF16)

    lam = _lam_of(lq1, lk1, lq2, lk2, lam_init)
    q = jnp.concatenate(_rope(q_ref[...].astype(F32), cosq_ref[...], sinaq_ref[...], sinbq_ref[...]), axis=-1)
    _diff_attn_heads(q,
                     lambda c: kt_ref[c * HEAD_DIM:(c + 1) * HEAD_DIM, :],
                     lambda h: vall_ref[:, h * V_DIM:(h + 1) * V_DIM],
                     lam, sg_ref[...], lam_init, o_ref)


def _rope_tables():
    t = jnp.arange(DEC_SEQ)
    row = (t // GRID_W).astype(F32)
    col = (t % GRID_W).astype(F32)
    half = HEAD_DIM // 2
    inv = ROPE_THETA ** (-jnp.arange(0, half, 2, dtype=F32) / half)
    ar = row[:, None] * inv
    ac = col[:, None] * inv
    ang = jnp.concatenate([ar, ar, ac, ac], axis=-1)
    cos, sin = jnp.cos(ang), jnp.sin(ang)
    first = (np.arange(HEAD_DIM) % half) < (half // 2)
    sin_a = jnp.where(first, -sin, 0.0)
    sin_b = jnp.where(first, 0.0, sin)
    return cos, sin_a, sin_b, cos.T, sin_a.T, sin_b.T


def _attn_lat(rest, kv_proj, cache_k, cache_v, rope, lam_q1, lam_k1, lam_q2, lam_k2, subln_g, layer):
    lam_init = 0.8 - 0.6 * math.exp(-0.3 * layer)
    tq = 256
    nq = DEC_SEQ // tq
    lv = _layer_vec_spec(HEAD_DIM, layer)
    qtab = pl.BlockSpec((tq, HEAD_DIM), lambda b, i: (i, 0))
    ktab = pl.BlockSpec((HEAD_DIM, DEC_SEQ), lambda b, i: (0, 0))
    cache = pl.BlockSpec((None, None, PAST_LEN, ATTN_W), lambda b, i: (b, layer, 0, 0))

    def kv(col):
        return pl.BlockSpec((DEC_SEQ, ATTN_W), lambda b, i: (P_ROWS // DEC_SEQ + b, col // ATTN_W))

    args = [rest, kv_proj, kv_proj, cache_k, cache_v, *rope, lam_q1, lam_k1, lam_q2, lam_k2, subln_g]
    return pl.pallas_call(
        functools.partial(_attn_lat_kernel, lam_init=lam_init),
        grid=(DEC_BATCH, nq),
        in_specs=[pl.BlockSpec((tq, ATTN_W), lambda b, i: (P_ROWS // tq + b * nq + i, COL_Q // ATTN_W)),
                  kv(KV_K), kv(KV_V), cache, cache, qtab, qtab, qtab, ktab, ktab, ktab,
                  lv, lv, lv, lv, _layer_vec_spec(V_DIM, layer)],
        out_specs=pl.BlockSpec((tq, ATTN_W), lambda b, i: (b * nq + i, 0)),
        out_shape=jax.ShapeDtypeStruct((S_ROWS, ATTN_W), BF16),
        scratch_shapes=[pltpu.VMEM((ATTN_W, PAST_LEN + DEC_SEQ), BF16),
                        pltpu.VMEM((PAST_LEN + DEC_SEQ, ATTN_W), BF16)],
        compiler_params=_params(2),
        name="attn_lat",
    )(*args)


def kernel(x_prompt, x_sample, cache_k, cache_v, c, c_ctx, w_ada, b_ada, g_pre_mix, g_post_mix,
           g_pre_ffn, g_post_ffn, w_in, conv_a_w, conv_a_b, ln_a_g, ln_a_b, w_a_out,
           lam_q1, lam_k1, lam_q2, lam_k2, subln_g, w_b_out, conv_c_w, w_c_out, w_d_out,
           w_gate, b_gate, w_o, w_ffn_up, w_ffn_down):
    cs = jnp.concatenate([c_ctx[None, :], c, jnp.zeros((ADA_ROWS - N_GROUPS, D_MODEL), F32)], axis=0)

    vec = lambda a: a.reshape(DEPTH, 1, a.shape[-1])
    g_pre_mix, g_post_mix, g_pre_ffn, g_post_ffn = map(vec, (g_pre_mix, g_post_mix, g_pre_ffn, g_post_ffn))
    conv_a_b, ln_a_g, ln_a_b, subln_g, b_gate, b_ada = map(vec, (conv_a_b, ln_a_g, ln_a_b, subln_g, b_gate, b_ada))
    mod = _mod_vectors(_ada(cs, w_ada, b_ada, 0))
    lam_q1, lam_k1, lam_q2, lam_k2 = map(vec, (lam_q1, lam_k1, lam_q2, lam_k2))
    cache_k = cache_k.reshape(DEC_BATCH, DEPTH, PAST_LEN, ATTN_W)
    cache_v = cache_v.reshape(DEC_BATCH, DEPTH, PAST_LEN, ATTN_W)
    rope = _rope_tables()
    dft = _fourier_tables()

    x, h = _normmod(x_prompt.reshape(P_ROWS, D_MODEL), x_sample.reshape(S_ROWS, D_MODEL),
                    g_pre_mix, mod, 0, 1, 0)
    new_k = jnp.zeros((BATCH, DEPTH, SEQ, N_HEADS, V_DIM), F32)
    new_v = jnp.zeros((BATCH, DEPTH, SEQ, N_HEADS, V_DIM), F32)
    for l in range(DEPTH):
        kv = _mm(h, w_in, l, tm=1024, tn=IN_TILE, out_dtype=F32, col_tiles=KV_TILES, name="mm_in_kv")
        rest = _mm(h, w_in, l, tm=1024, tn=IN_TILE, out_dtype=BF16, col_tiles=REST_TILES, name="mm_in")
        if l + 1 < DEPTH:
            gates, ada_next = _mm(h, w_gate, l, tm=1024, tn=1024, out_dtype=BF16, bias=b_gate,
                                  ada=(cs, w_ada, b_ada, l + 1), name="mm_gate")
            mod_next = _mod_vectors(ada_next)
        else:
            gates = _mm(h, w_gate, l, tm=1024, tn=1024, out_dtype=BF16, bias=b_gate, name="mm_gate")

        branch_args = (rest, conv_a_w, conv_a_b, ln_a_g, ln_a_b, conv_c_w, dft, l)
        ctx = _branches(*branch_args, SEQ, BATCH, 0)
        lat = _branches(*branch_args, DEC_SEQ, DEC_BATCH, P_ROWS)
        ya, yc, yd = zip(ctx, lat)
        lam_args = (lam_q1, lam_k1, lam_q2, lam_k2, subln_g, l)
        yb_ctx, new_k, new_v = _attn_ctx(rest, kv, *lam_args, new_k, new_v)
        yb = (yb_ctx, _attn_lat(rest, kv, cache_k, cache_v, rope, *lam_args))

        merged = _merge((ya, yb, yc, yd), gates, (w_a_out, w_b_out, w_c_out, w_d_out), l)
        x, h2 = _mm_resid(merged, w_o, x, mod, l, 2, g_post_mix, (g_pre_ffn, l, 4, 3, mod),
                          tm=512, piece=128, chunk=512, name="mm_o")
        act = _ffn_up(h2, w_ffn_up, l)
        nxt = (g_pre_mix, l + 1, 1, 0, mod_next) if l + 1 < DEPTH else None
        res = _mm_resid(act, w_ffn_down, x, mod, l, 5, g_post_ffn, nxt,
                        tm=256, piece=128, chunk=512, name="mm_down")
        if nxt is not None:
            x, h = res
            mod = mod_next

    y_ctx, y_lat = res
    return y_ctx.reshape(BATCH, SEQ, D_MODEL), y_lat.reshape(DEC_BATCH, DEC_SEQ, D_MODEL), new_k, new_v
```

```python
import functools
import math

import numpy as np
import jax
import jax.numpy as jnp
from jax import lax
from jax.experimental import pallas as pl
from jax.experimental.pallas import tpu as pltpu

D_MODEL = 2048
BATCH = 16
SEQ = 256
DEPTH = 4
DEC_BATCH = 2
DEC_SEQ = 1024
PAST_LEN = 512
GRID_W = 64
N_HEADS = 4
HEAD_DIM = 128
V_DIM = 2 * HEAD_DIM
ATTN_W = N_HEADS * 2 * HEAD_DIM
CONV_W = 512
CONV_K = 31
SC_W = 512
SC_K = 3
F_W = 512
F_GROUPS = 4
F_GROUP = F_W // F_GROUPS
N_BRANCH = 4
D_IN = 2 * CONV_W + 3 * ATTN_W + 3 * SC_W + F_W
D_FF = ((8 * D_MODEL + 3 * 256 - 1) // (3 * 256)) * 256
ROPE_THETA = 10000.0
EPS = 1e-6

P_ROWS = BATCH * SEQ
S_ROWS = DEC_BATCH * DEC_SEQ
N_ROWS = P_ROWS + S_ROWS
N_GROUPS = 1 + DEC_BATCH
N_MOD = 6
ADA_ROWS = 8

IN_TILE = 1024
KV_TILES = (2, 3)
REST_TILES = (0, 1, 4, 5)
KV_K = 0
KV_V = ATTN_W
COL_A = 0
COL_Q = 2 * CONV_W
COL_SB = COL_Q + ATTN_W
COL_SC = COL_SB + SC_W
COL_SX = COL_SC + SC_W
COL_D = COL_SX + SC_W

SUBLANES = 8
CONV_PAD = 16
CONV_ROWS = 64
VMEM_LIMIT = 56 * 1024 * 1024
MM_PIECE = 256

F32 = jnp.float32
BF16 = jnp.bfloat16


def _params(n_axes, vmem=VMEM_LIMIT):
    return pltpu.CompilerParams(dimension_semantics=("arbitrary",) * n_axes, vmem_limit_bytes=vmem)


def _group_of_tile(i, tm):
    pt = P_ROWS // tm
    st = DEC_SEQ // tm
    return jnp.maximum(i - (pt - st), 0) // st


def _mod_spec(which, tm, tile_of):
    def index(*ids):
        return (_group_of_tile(tile_of(*ids), tm) * N_MOD + which, 0, 0)
    return pl.BlockSpec((None, 1, D_MODEL), index)


def _mod_vectors(ada_rows):
    return ada_rows[:N_GROUPS].reshape(N_GROUPS * N_MOD, 1, D_MODEL)


def _layer_vec_spec(width, layer):
    return pl.BlockSpec((None, 1, width), lambda *ids: (layer, 0, 0))


def _ada_chunk(c_ref, w_ref, b_ref, o_ref):
    s = _silu(c_ref[...]).astype(BF16)
    o_ref[...] = jnp.dot(s, w_ref[...].astype(BF16), preferred_element_type=F32) + b_ref[...]


def _ada(cs, w_ada, b_ada, layer):
    tn = 1024
    n = N_MOD * D_MODEL
    return pl.pallas_call(
        _ada_chunk,
        grid=(n // tn,),
        in_specs=[
            pl.BlockSpec((ADA_ROWS, D_MODEL), lambda j: (0, 0)),
            pl.BlockSpec((None, D_MODEL, tn), lambda j: (layer, 0, j)),
            pl.BlockSpec((None, 1, tn), lambda j: (layer, 0, j)),
        ],
        out_specs=pl.BlockSpec((ADA_ROWS, tn), lambda j: (0, j)),
        out_shape=jax.ShapeDtypeStruct((ADA_ROWS, n), F32),
        compiler_params=_params(1),
        name="ada",
    )(cs, w_ada, b_ada)


def _rms(x):
    return x * lax.rsqrt(jnp.mean(x * x, axis=-1, keepdims=True) + EPS)


def _sigmoid(x):
    return 0.5 * jnp.tanh(0.5 * x) + 0.5


def _silu(x):
    return x * _sigmoid(x)


def _normmod_kernel(xc_ref, xl_ref, g_ref, sc_ref, sh_ref, x_ref, h_ref, *, ctx_tiles):
    def tile(src_ref):
        x = src_ref[...]
        x_ref[...] = x
        y = _rms(x) * g_ref[...]
        h_ref[...] = (y * (1.0 + sc_ref[...]) + sh_ref[...]).astype(BF16)

    is_ctx = pl.program_id(0) < ctx_tiles
    pl.when(is_ctx)(lambda: tile(xc_ref))
    pl.when(jnp.logical_not(is_ctx))(lambda: tile(xl_ref))


def _normmod(x_ctx, x_lat, g, mod, layer, which_sc, which_sh):
    tm = 256
    ctx_tiles = P_ROWS // tm
    row = pl.BlockSpec((tm, D_MODEL), lambda i: (i, 0))
    tile = lambda i: i
    return pl.pallas_call(
        functools.partial(_normmod_kernel, ctx_tiles=ctx_tiles),
        grid=(N_ROWS // tm,),
        in_specs=[pl.BlockSpec((tm, D_MODEL), lambda i: (jnp.minimum(i, ctx_tiles - 1), 0)),
                  pl.BlockSpec((tm, D_MODEL), lambda i: (jnp.maximum(i - ctx_tiles, 0), 0)),
                  _layer_vec_spec(D_MODEL, layer),
                  _mod_spec(which_sc, tm, tile), _mod_spec(which_sh, tm, tile)],
        out_specs=[row, row],
        out_shape=[jax.ShapeDtypeStruct((N_ROWS, D_MODEL), F32), jax.ShapeDtypeStruct((N_ROWS, D_MODEL), BF16)],
        compiler_params=_params(1),
        name="normmod",
    )(x_ctx, x_lat, g, mod, mod)


def _cast_weight(w_ref, wbf_ref):
    rows = w_ref.shape[0]
    chunk = 256

    def body(r, carry):
        r0 = pl.multiple_of(r * chunk, chunk)
        wbf_ref[pl.ds(r0, chunk), :] = w_ref[pl.ds(r0, chunk), :].astype(BF16)
        return carry

    lax.fori_loop(0, rows // chunk, body, 0)


def _mm_pieces(tm, tn, split_rows):
    cols = [slice(c, c + MM_PIECE) for c in range(0, tn, MM_PIECE)]
    if not split_rows:
        return [(slice(0, tm), c) for c in cols]
    pieces = [(slice(0, tm), c) for c in cols[:-2]]
    for rows in (slice(0, tm // 2), slice(tm // 2, tm)):
        pieces += [(rows, c) for c in cols[-2:]]
    return pieces


def _mm_kernel(x_ref, w_ref, *rest, sigmoid_bias, with_ada):
    rest = list(rest)
    b_ref = rest.pop(0) if sigmoid_bias else None
    if with_ada:
        c_ref, wa_ref, ba_ref, o_ref, mod_ref, wbf_ref = rest
    else:
        o_ref, wbf_ref = rest

    @pl.when(pl.program_id(1) == 0)
    def _():
        _cast_weight(w_ref, wbf_ref)

    x = x_ref[...]
    for n, (rows, cols) in enumerate(_mm_pieces(*o_ref.shape, split_rows=False)):
        acc = jnp.dot(x, wbf_ref[:, cols], preferred_element_type=F32)
        if sigmoid_bias:
            acc = _sigmoid(acc + b_ref[:, cols])
        o_ref[rows, cols] = acc.astype(o_ref.dtype)
        if with_ada and n == 0:
            _ada_chunk(c_ref, wa_ref, ba_ref, mod_ref)


def _mm(x, w, layer, *, tm, tn, out_dtype, col_tiles=None, bias=None, ada=None, name):
    m, k = x.shape
    if col_tiles is None:
        col_tiles = tuple(range(w.shape[2] // tn))
    n = len(col_tiles) * tn
    steps = (n // tn) * (m // tm)

    def wcol(j):
        return sum((j == idx) * tile for idx, tile in enumerate(col_tiles))

    in_specs = [pl.BlockSpec((tm, k), lambda j, i: (i, 0)),
                pl.BlockSpec((None, k, tn), lambda j, i: (layer, 0, wcol(j)))]
    args = [x, w]
    out_specs = [pl.BlockSpec((tm, tn), lambda j, i: (i, j))]
    out_shape = [jax.ShapeDtypeStruct((m, n), out_dtype)]
    if bias is not None:
        in_specs.append(pl.BlockSpec((None, 1, tn), lambda j, i: (layer, 0, wcol(j))))
        args.append(bias)
    if ada is not None:
        cs, w_ada, b_ada, layer_a = ada
        n_ada = w_ada.shape[2]
        ta = n_ada // steps
        assert ta * steps == n_ada and ta % 128 == 0
        step = lambda j, i: j * (m // tm) + i
        in_specs += [pl.BlockSpec((ADA_ROWS, D_MODEL), lambda j, i: (0, 0)),
                     pl.BlockSpec((None, D_MODEL, ta), lambda j, i: (layer_a, 0, step(j, i))),
                     pl.BlockSpec((None, 1, ta), lambda j, i: (layer_a, 0, step(j, i)))]
        args += [cs, w_ada, b_ada]
        out_specs.append(pl.BlockSpec((ADA_ROWS, ta), lambda j, i: (0, step(j, i))))
        out_shape.append(jax.ShapeDtypeStruct((ADA_ROWS, n_ada), F32))
    res = pl.pallas_call(
        functools.partial(_mm_kernel, sigmoid_bias=bias is not None, with_ada=ada is not None),
        grid=(n // tn, m // tm),
        in_specs=in_specs,
        out_specs=out_specs,
        out_shape=out_shape,
        scratch_shapes=[pltpu.VMEM((k, tn), BF16)],
        compiler_params=_params(2),
        name=name,
    )(*args)
    return res if ada is not None else res[0]


def _merge_kernel(*refs, ctx_tiles):
    ctx_branches, lat_branches = refs[0:N_BRANCH], refs[N_BRANCH:2 * N_BRANCH]
    gate_refs = refs[2 * N_BRANCH:3 * N_BRANCH]
    w_refs = refs[3 * N_BRANCH:4 * N_BRANCH]
    o_ref = refs[4 * N_BRANCH]
    w_scratch = refs[4 * N_BRANCH + 1:]

    @pl.when(pl.program_id(1) == 0)
    def _():
        for w_ref, w_s in zip(w_refs, w_scratch):
            _cast_weight(w_ref, w_s)

    def merge(branches):
        acc = None
        for y_ref, g_ref, w_s in zip(branches, gate_refs, w_scratch):
            term = g_ref[...].astype(F32) * jnp.dot(y_ref[...], w_s[...], preferred_element_type=F32)
            acc = term if acc is None else acc + term
        o_ref[...] = acc.astype(BF16)

    is_ctx = pl.program_id(1) < ctx_tiles
    pl.when(is_ctx)(lambda: merge(ctx_branches))
    pl.when(jnp.logical_not(is_ctx))(lambda: merge(lat_branches))


def _merge(branches, gates, weights, layer):
    tm, tn = 512, 1024
    nt = D_MODEL // tn
    ctx_tiles = P_ROWS // tm
    widths = [w.shape[1] for w in weights]

    def ctx(width):
        return pl.BlockSpec((tm, width), lambda j, i: (jnp.minimum(i, ctx_tiles - 1), 0))

    def lat(width):
        return pl.BlockSpec((tm, width), lambda j, i: (jnp.maximum(i - ctx_tiles, 0), 0))

    def gate(branch):
        return pl.BlockSpec((tm, tn), lambda j, i: (i, branch * nt + j))

    def wt(width):
        return pl.BlockSpec((None, width, tn), lambda j, i: (layer, 0, j))

    return pl.pallas_call(
        functools.partial(_merge_kernel, ctx_tiles=ctx_tiles),
        grid=(nt, N_ROWS // tm),
        in_specs=([ctx(w) for w in widths] + [lat(w) for w in widths]
                  + [gate(b) for b in range(N_BRANCH)] + [wt(w) for w in widths]),
        out_specs=pl.BlockSpec((tm, tn), lambda j, i: (i, j)),
        out_shape=jax.ShapeDtypeStruct((N_ROWS, D_MODEL), BF16),
        scratch_shapes=[pltpu.VMEM((w, tn), BF16) for w in widths],
        compiler_params=_params(2),
        name="merge",
    )(*[b[0] for b in branches], *[b[1] for b in branches], *([gates] * N_BRANCH), *weights)


def _ffn_up_kernel(x_ref, wu_ref, wg_ref, o_ref, wu_s, wg_s):
    @pl.when(pl.program_id(1) == 0)
    def _():
        _cast_weight(wu_ref, wu_s)
        _cast_weight(wg_ref, wg_s)

    for rows, cols in _mm_pieces(*o_ref.shape, split_rows=True):
        x = x_ref[rows, :]
        up = jnp.dot(x, wu_s[:, cols], preferred_element_type=F32)
        gt = jnp.dot(x, wg_s[:, cols], preferred_element_type=F32)
        o_ref[rows, cols] = (_silu(gt) * up).astype(BF16)


def _ffn_up(h, w_up, layer):
    tm, tn = 2048, 512
    nt = D_FF // tn
    return pl.pallas_call(
        _ffn_up_kernel,
        grid=(nt, N_ROWS // tm),
        in_specs=[pl.BlockSpec((tm, D_MODEL), lambda j, i: (i, 0)),
                  pl.BlockSpec((None, D_MODEL, tn), lambda j, i: (layer, 0, j)),
                  pl.BlockSpec((None, D_MODEL, tn), lambda j, i: (layer, 0, nt + j))],
        out_specs=pl.BlockSpec((tm, tn), lambda j, i: (i, j)),
        out_shape=jax.ShapeDtypeStruct((N_ROWS, D_FF), BF16),
        scratch_shapes=[pltpu.VMEM((D_MODEL, tn), BF16), pltpu.VMEM((D_MODEL, tn), BF16)],
        compiler_params=_params(2),
        name="ffn_up",
    )(h, w_up, w_up)


EPI_ROWS = 16


def _mm_resid_kernel(*refs, nw, chunk, tm, piece, with_next):
    if with_next:
        a_ref, w_ref, x_ref, gate_ref, gpost_ref, gpre_ref, sc_ref, sh_ref, xo_ref, h_ref, wbf_ref = refs
    else:
        a_ref, w_ref, x_ref, gate_ref, gpost_ref, yctx_ref, ylat_ref, wbf_ref = refs
    s = pl.program_id(0)

    @pl.when(s < nw)
    def _():
        r0 = pl.multiple_of(s * chunk, chunk)
        wbf_ref[pl.ds(r0, chunk), :] = w_ref[...].astype(BF16)

    def row_tile(xo_ref):
        gg = gate_ref[...] * gpost_ref[...]
        if with_next:
            pp = gpre_ref[...] * (1.0 + sc_ref[...])
            sh = sh_ref[...]
        for p0 in range(0, tm, piece):
            f = jnp.dot(a_ref[p0:p0 + piece, :], wbf_ref[...], preferred_element_type=F32)
            for r in range(0, piece, EPI_ROWS):
                rows = slice(p0 + r, p0 + r + EPI_ROWS)
                xn = x_ref[rows, :] + _rms(f[r:r + EPI_ROWS, :]) * gg
                xo_ref[rows, :] = xn
                if with_next:
                    h_ref[rows, :] = (_rms(xn) * pp + sh).astype(BF16)

    if with_next:
        pl.when(s >= nw)(lambda: row_tile(xo_ref))
    else:
        first_lat = nw + P_ROWS // tm
        pl.when((s >= nw) & (s < first_lat))(lambda: row_tile(yctx_ref))
        pl.when(s >= first_lat)(lambda: row_tile(ylat_ref))


def _mm_resid(a, w, x, mod, layer, which_gate, g_post, nxt, *, tm, piece, chunk, name):
    k = a.shape[1]
    nw = k // chunk
    nt = N_ROWS // tm
    ctx_tiles = P_ROWS // tm
    tile = lambda s: jnp.maximum(s - nw, 0)
    row = pl.BlockSpec((tm, D_MODEL), lambda s: (tile(s), 0))
    in_specs = [pl.BlockSpec((tm, k), lambda s: (tile(s), 0)),
                pl.BlockSpec((None, chunk, D_MODEL), lambda s: (layer, jnp.minimum(s, nw - 1), 0)),
                row, _mod_spec(which_gate, tm, tile), _layer_vec_spec(D_MODEL, layer)]
    args = [a, w, x, mod, g_post]
    if nxt is not None:
        g_pre, nl, wsc, wsh, mod_next = nxt
        in_specs += [_layer_vec_spec(D_MODEL, nl), _mod_spec(wsc, tm, tile), _mod_spec(wsh, tm, tile)]
        args += [g_pre, mod_next, mod_next]
        out_specs = [row, row]
        out_shape = [jax.ShapeDtypeStruct((N_ROWS, D_MODEL), F32), jax.ShapeDtypeStruct((N_ROWS, D_MODEL), BF16)]
    else:
        out_specs = [pl.BlockSpec((tm, D_MODEL), lambda s: (jnp.minimum(tile(s), ctx_tiles - 1), 0)),
                     pl.BlockSpec((tm, D_MODEL), lambda s: (jnp.maximum(tile(s) - ctx_tiles, 0), 0))]
        out_shape = [jax.ShapeDtypeStruct((P_ROWS, D_MODEL), F32), jax.ShapeDtypeStruct((S_ROWS, D_MODEL), F32)]
    return pl.pallas_call(
        functools.partial(_mm_resid_kernel, nw=nw, chunk=chunk, tm=tm, piece=piece, with_next=nxt is not None),
        grid=(nw + nt,),
        in_specs=in_specs,
        out_specs=out_specs,
        out_shape=out_shape,
        scratch_shapes=[pltpu.VMEM((k, D_MODEL), BF16)],
        compiler_params=_params(1),
        name=name,
    )(*args)


SEQ_BLOCK = DEC_SEQ


def _seq_spec(rows, width, col, row0):
    return pl.BlockSpec((rows, width), lambda b: (row0 // rows + b, col // width))


def _dwconv_chunk(pad_ref, w_ref, r0, ksize):
    win = CONV_ROWS + 2 * CONV_PAD
    window = pad_ref[r0:r0 + win, :]
    acc = None
    for b in range(SUBLANES):
        taps = [j for j in range(ksize) if (CONV_PAD - ksize // 2 + j) % SUBLANES == b]
        if not taps:
            continue
        shifted = window if b == 0 else pltpu.roll(window, win - b, 0)
        for j in taps:
            a0 = (CONV_PAD - ksize // 2 + j) - b
            term = w_ref[j:j + 1, :] * shifted[a0:a0 + CONV_ROWS, :]
            acc = term if acc is None else acc + term
    return acc


def _fill_pad(pad_ref, u, t):
    zeros = jnp.zeros((CONV_PAD, pad_ref.shape[1]), F32)
    pad_ref[0:CONV_PAD, :] = zeros
    pad_ref[CONV_PAD + t:2 * CONV_PAD + t, :] = zeros
    pad_ref[CONV_PAD:CONV_PAD + t, :] = u


def _conv_a_kernel(a_ref, w_ref, cb_ref, lg_ref, lb_ref, o_ref, pad_ref, *, t):
    for base in range(0, SEQ_BLOCK, t):
        seq = slice(base, base + t)
        a_val = a_ref[seq, 0:CONV_W].astype(F32)
        _fill_pad(pad_ref, a_val * _sigmoid(a_ref[seq, CONV_W:2 * CONV_W].astype(F32)), t)
        for r0 in range(0, t, CONV_ROWS):
            acc = _dwconv_chunk(pad_ref, w_ref, r0, CONV_K) + cb_ref[...]
            mu = jnp.mean(acc, axis=-1, keepdims=True)
            xc = acc - mu
            y = xc * lax.rsqrt(jnp.mean(xc * xc, axis=-1, keepdims=True) + EPS)
            y = y * lg_ref[...] + lb_ref[...]
            o_ref[base + r0:base + r0 + CONV_ROWS, :] = _silu(y).astype(BF16)


def _conv_c_kernel(sb_ref, sc_ref, sx_ref, w_ref, o_ref, pad_ref, *, t):
    for base in range(0, SEQ_BLOCK, t):
        seq = slice(base, base + t)
        _fill_pad(pad_ref, sc_ref[seq, :].astype(F32) * sx_ref[seq, :].astype(F32), t)
        for r0 in range(0, t, CONV_ROWS):
            rows = slice(base + r0, base + r0 + CONV_ROWS)
            gated = sb_ref[rows, :].astype(F32) * _dwconv_chunk(pad_ref, w_ref, r0, SC_K)
            o_ref[rows, :] = gated.astype(BF16)


def _fourier_kernel(u_ref, cc_ref, sc_ref, ct_ref, st_ref, o_ref):
    t = ct_ref.shape[0]
    for base in range(0, SEQ_BLOCK, t):
        seq = slice(base, base + t)
        u = u_ref[seq, :].astype(BF16)
        yc = jnp.dot(u, cc_ref[...], preferred_element_type=F32).astype(BF16)
        ys = jnp.dot(u, sc_ref[...], preferred_element_type=F32).astype(BF16)
        f = (jnp.dot(ct_ref[...], yc, preferred_element_type=F32)
             - jnp.dot(st_ref[...], ys, preferred_element_type=F32))
        o_ref[seq, :] = f.astype(BF16)


def _dft_tables(n):
    idx = np.arange(n)
    ang = 2.0 * np.pi * ((idx[:, None] * idx[None, :]) % n) / n
    s = 1.0 / math.sqrt(n)
    return (np.cos(ang) * s).astype(np.float32), (np.sin(ang) * s).astype(np.float32)


def _fourier_tables():
    cg, sg = _dft_tables(F_GROUP)
    eye = np.eye(F_GROUPS, dtype=np.float32)
    tabs = {"chan": (np.kron(eye, cg), np.kron(eye, sg)), SEQ: _dft_tables(SEQ), DEC_SEQ: _dft_tables(DEC_SEQ)}
    return {k: tuple(jnp.asarray(m).astype(BF16) for m in v) for k, v in tabs.items()}


def _branches_kernel(a_ref, sb_ref, sc_ref, sx_ref, d_ref, wa_ref, cb_ref, lg_ref, lb_ref, wc_ref,
                     cc_ref, cs_ref, ct_ref, st_ref, ya_ref, yc_ref, yd_ref, pad_a, pad_c, *, t):
    _conv_a_kernel(a_ref, wa_ref, cb_ref, lg_ref, lb_ref, ya_ref, pad_a, t=t)
    _conv_c_kernel(sb_ref, sc_ref, sx_ref, wc_ref, yc_ref, pad_c, t=t)
    _fourier_kernel(d_ref, cc_ref, cs_ref, ct_ref, st_ref, yd_ref)


def _branches(rest, conv_a_w, conv_a_b, ln_a_g, ln_a_b, conv_c_w, tabs, layer, t, nseq, row0):
    full = lambda n: pl.BlockSpec((n, n), lambda b: (0, 0))
    lv = _layer_vec_spec(CONV_W, layer)
    in_specs = [_seq_spec(SEQ_BLOCK, 2 * CONV_W, COL_A, row0),
                _seq_spec(SEQ_BLOCK, SC_W, COL_SB, row0), _seq_spec(SEQ_BLOCK, SC_W, COL_SC, row0),
                _seq_spec(SEQ_BLOCK, SC_W, COL_SX, row0), _seq_spec(SEQ_BLOCK, F_W, COL_D, row0),
                pl.BlockSpec((None, CONV_K, CONV_W), lambda b: (layer, 0, 0)), lv, lv, lv,
                pl.BlockSpec((None, SC_K, SC_W), lambda b: (layer, 0, 0)),
                full(F_W), full(F_W), full(t), full(t)]
    out = pl.BlockSpec((SEQ_BLOCK, CONV_W), lambda b: (b, 0))
    out_shape = jax.ShapeDtypeStruct((nseq * t, CONV_W), BF16)
    return pl.pallas_call(
        functools.partial(_branches_kernel, t=t),
        grid=(nseq * t // SEQ_BLOCK,),
        in_specs=in_specs,
        out_specs=[out, out, out],
        out_shape=[out_shape, out_shape, out_shape],
        scratch_shapes=[pltpu.VMEM((t + 2 * CONV_PAD, CONV_W), F32), pltpu.VMEM((t + 2 * CONV_PAD, SC_W), F32)],
        compiler_params=_params(1),
        name="branches",
    )(rest, rest, rest, rest, rest, conv_a_w, conv_a_b, ln_a_g, ln_a_b, conv_c_w, *tabs["chan"], *tabs[t])


def _lam_of(lq1_ref, lk1_ref, lq2_ref, lk2_ref, lam_init):
    s1 = jnp.sum(lq1_ref[...] * lk1_ref[...], axis=-1, keepdims=True)
    s2 = jnp.sum(lq2_ref[...] * lk2_ref[...], axis=-1, keepdims=True)
    return jnp.exp(s1) - jnp.exp(s2) + lam_init


LOGIT_SCALE = HEAD_DIM ** -0.5 * math.log2(math.e)


def _exp2_and_rsum(t):
    e = jnp.exp2(t - jnp.max(t, axis=-1, keepdims=True))
    return e, 1.0 / jnp.sum(e, axis=-1, keepdims=True)


def _transposed_chunks(k):
    return [k[:, c * HEAD_DIM:(c + 1) * HEAD_DIM].T.astype(BF16) for c in range(ATTN_W // HEAD_DIM)]


def _diff_attn_heads(q, kt_of, v_of, lam, subln, lam_init, o_ref):
    def logits(h):
        maps = []
        for m in range(2):
            c = 2 * h + m
            qc = (q[:, c * HEAD_DIM:(c + 1) * HEAD_DIM] * LOGIT_SCALE).astype(BF16)
            maps.append(jnp.dot(qc, kt_of(c), preferred_element_type=F32))
        return maps

    pending = logits(0)
    for h in range(N_HEADS):
        following = logits(h + 1) if h + 1 < N_HEADS else None
        (e1, r1), (e2, r2) = (_exp2_and_rsum(t) for t in pending)
        a = (e1 * r1 - e2 * (lam * r2)).astype(BF16)
        o = jnp.dot(a, v_of(h), preferred_element_type=F32)
        o = _rms(o) * subln * (1.0 - lam_init)
        o_ref[:, h * V_DIM:(h + 1) * V_DIM] = o.astype(BF16)
        pending = following


def _attn_ctx_kernel(q_ref, k_ref, v_ref, lq1, lk1, lq2, lk2, sg_ref, nk_in, nv_in, o_ref, nk_ref, nv_ref, *,
                     lam_init):
    lam = _lam_of(lq1, lk1, lq2, lk2, lam_init)
    k = k_ref[...]
    v = v_ref[...]
    for h in range(N_HEADS):
        nk_ref[:, h, :] = k[:, h * V_DIM:(h + 1) * V_DIM]
        nv_ref[:, h, :] = v[:, h * V_DIM:(h + 1) * V_DIM]
    kt =_transposed_chunks(k)
    vb = v.astype(BF16)
    _diff_attn_heads(q_ref[...].astype(F32),
                     lambda c: kt[c],
                     lambda h: vb[:, h * V_DIM:(h + 1) * V_DIM],
                     lam, sg_ref[...], lam_init, o_ref)


def _attn_ctx(rest, kv, lam_q1, lam_k1, lam_q2, lam_k2, subln_g, layer, new_k, new_v):
    lam_init = 0.8 - 0.6 * math.exp(-0.3 * layer)
    lv = _layer_vec_spec(HEAD_DIM, layer)
    hbm = pl.BlockSpec(memory_space=pl.ANY)
    in_specs = [_seq_spec(SEQ, ATTN_W, COL_Q, 0), _seq_spec(SEQ, ATTN_W, KV_K, 0),
                _seq_spec(SEQ, ATTN_W, KV_V, 0), lv, lv, lv, lv, _layer_vec_spec(V_DIM, layer), hbm, hbm]
    args = [rest, kv, kv, lam_q1, lam_k1, lam_q2, lam_k2, subln_g, new_k, new_v]
    kv_spec = pl.BlockSpec((None, None, SEQ, N_HEADS, V_DIM), lambda b: (b, layer, 0, 0, 0))
    kv_shape = jax.ShapeDtypeStruct(new_k.shape, F32)
    return pl.pallas_call(
        functools.partial(_attn_ctx_kernel, lam_init=lam_init),
        grid=(BATCH,),
        in_specs=in_specs,
        out_specs=[pl.BlockSpec((SEQ, ATTN_W), lambda b: (b, 0)), kv_spec, kv_spec],
        out_shape=[jax.ShapeDtypeStruct((P_ROWS, ATTN_W), BF16), kv_shape, kv_shape],
        input_output_aliases={len(args) - 2: 1, len(args) - 1: 2},
        compiler_params=_params(1),
        name="attn_ctx",
    )(*args)


def _rope(x, cos, sin_a, sin_b):
    outs = []
    for c in range(ATTN_W // HEAD_DIM):
        xc = x[:, c * HEAD_DIM:(c + 1) * HEAD_DIM]
        fwd = pltpu.roll(xc, HEAD_DIM - HEAD_DIM // 4, 1)
        bwd = pltpu.roll(xc, HEAD_DIM // 4, 1)
        outs.append(xc * cos + fwd * sin_a + bwd * sin_b)
    return outs


def _attn_lat_kernel(q_ref, k_ref, v_ref, ck_ref, cv_ref, cosq_ref, sinaq_ref, sinbq_ref,
                     cos_ref, sina_ref, sinb_ref, lq1, lk1, lq2, lk2, sg_ref, o_ref,
                     kt_ref, vall_ref, *, lam_init):
    @pl.when(pl.program_id(1) == 0)
    def _():
        vall_ref[0:PAST_LEN, :] = cv_ref[...].astype(BF16)
        vall_ref[PAST_LEN:PAST_LEN + DEC_SEQ, :] = v_ref[...].astype(BF16)
        for c, kc in enumerate(_transposed_chunks(ck_ref[...])):
            kt_ref[c * HEAD_DIM:(c + 1) * HEAD_DIM, 0:PAST_LEN] = kc
        for c in range(ATTN_W // HEAD_DIM):
            kt = k_ref[:, c * HEAD_DIM:(c + 1) * HEAD_DIM].T
            fwd = pltpu.roll(kt, HEAD_DIM - HEAD_DIM // 4, 0)
            bwd = pltpu.roll(kt, HEAD_DIM // 4, 0)
            kr = kt * cos_ref[...] + fwd * sina_ref[...] + bwd * sinb_ref[...]
            kt_ref[c * HEAD_DIM:(c + 1) * HEAD_DIM, PAST_LEN:PAST_LEN + DEC_SEQ] = kr.astype(BF16)

    lam = _lam_of(lq1, lk1, lq2, lk2, lam_init)
    q = jnp.concatenate(_rope(q_ref[...].astype(F32), cosq_ref[...], sinaq_ref[...], sinbq_ref[...]), axis=-1)
    _diff_attn_heads(q,
                     lambda c: kt_ref[c * HEAD_DIM:(c + 1) * HEAD_DIM, :],
                     lambda h: vall_ref[:, h * V_DIM:(h + 1) * V_DIM],
                     lam, sg_ref[...], lam_init, o_ref)


def _rope_tables():
    t = jnp.arange(DEC_SEQ)
    row = (t // GRID_W).astype(F32)
    col = (t % GRID_W).astype(F32)
    half = HEAD_DIM // 2
    inv = ROPE_THETA ** (-jnp.arange(0, half, 2, dtype=F32) / half)
    ar = row[:, None] * inv
    ac = col[:, None] * inv
    ang = jnp.concatenate([ar, ar, ac, ac], axis=-1)
    cos, sin = jnp.cos(ang), jnp.sin(ang)
    first = (np.arange(HEAD_DIM) % half) < (half // 2)
    sin_a = jnp.where(first, -sin, 0.0)
    sin_b = jnp.where(first, 0.0, sin)
    return cos, sin_a, sin_b, cos.T, sin_a.T, sin_b.T


def _attn_lat(rest, kv_proj, cache_k, cache_v, rope, lam_q1, lam_k1, lam_q2, lam_k2, subln_g, layer):
    lam_init = 0.8 - 0.6 * math.exp(-0.3 * layer)
    tq = 512
    nq = DEC_SEQ // tq
    lv = _layer_vec_spec(HEAD_DIM, layer)
    qtab = pl.BlockSpec((tq, HEAD_DIM), lambda b, i: (i, 0))
    ktab = pl.BlockSpec((HEAD_DIM, DEC_SEQ), lambda b, i: (0, 0))
    cache = pl.BlockSpec((None, None, PAST_LEN, ATTN_W), lambda b, i: (b, layer, 0, 0))

    def kv(col):
        return pl.BlockSpec((DEC_SEQ, ATTN_W), lambda b, i: (P_ROWS // DEC_SEQ + b, col // ATTN_W))

    args = [rest, kv_proj, kv_proj, cache_k, cache_v, *rope, lam_q1, lam_k1, lam_q2, lam_k2, subln_g]
    return pl.pallas_call(
        functools.partial(_attn_lat_kernel, lam_init=lam_init),
        grid=(DEC_BATCH, nq),
        in_specs=[pl.BlockSpec((tq, ATTN_W), lambda b, i: (P_ROWS // tq + b * nq + i, COL_Q // ATTN_W)),
                  kv(KV_K), kv(KV_V), cache, cache, qtab, qtab, qtab, ktab, ktab, ktab,
                  lv, lv, lv, lv, _layer_vec_spec(V_DIM, layer)],
        out_specs=pl.BlockSpec((tq, ATTN_W), lambda b, i: (b * nq + i, 0)),
        out_shape=jax.ShapeDtypeStruct((S_ROWS, ATTN_W), BF16),
        scratch_shapes=[pltpu.VMEM((ATTN_W, PAST_LEN + DEC_SEQ), BF16),
                        pltpu.VMEM((PAST_LEN + DEC_SEQ, ATTN_W), BF16)],
        compiler_params=_params(2),
        name="attn_lat",
    )(*args)


def kernel(x_prompt, x_sample, cache_k, cache_v, c, c_ctx, w_ada, b_ada, g_pre_mix, g_post_mix,
           g_pre_ffn, g_post_ffn, w_in, conv_a_w, conv_a_b, ln_a_g, ln_a_b, w_a_out,
           lam_q1, lam_k1, lam_q2, lam_k2, subln_g, w_b_out, conv_c_w, w_c_out, w_d_out,
           w_gate, b_gate, w_o, w_ffn_up, w_ffn_down):
    cs = jnp.concatenate([c_ctx[None, :], c, jnp.zeros((ADA_ROWS - N_GROUPS, D_MODEL), F32)], axis=0)

    vec = lambda a: a.reshape(DEPTH, 1, a.shape[-1])
    g_pre_mix, g_post_mix, g_pre_ffn, g_post_ffn = map(vec, (g_pre_mix, g_post_mix, g_pre_ffn, g_post_ffn))
    conv_a_b, ln_a_g, ln_a_b, subln_g, b_gate, b_ada = map(vec, (conv_a_b, ln_a_g, ln_a_b, subln_g, b_gate, b_ada))
    mod = _mod_vectors(_ada(cs, w_ada, b_ada, 0))
    lam_q1, lam_k1, lam_q2, lam_k2 = map(vec, (lam_q1, lam_k1, lam_q2, lam_k2))
    cache_k = cache_k.reshape(DEC_BATCH, DEPTH, PAST_LEN, ATTN_W)
    cache_v = cache_v.reshape(DEC_BATCH, DEPTH, PAST_LEN, ATTN_W)
    rope = _rope_tables()
    dft = _fourier_tables()

    x, h = _normmod(x_prompt.reshape(P_ROWS, D_MODEL), x_sample.reshape(S_ROWS, D_MODEL),
                    g_pre_mix, mod, 0, 1, 0)
    new_k = jnp.zeros((BATCH, DEPTH, SEQ, N_HEADS, V_DIM), F32)
    new_v = jnp.zeros((BATCH, DEPTH, SEQ, N_HEADS, V_DIM), F32)
    for l in range(DEPTH):
        kv = _mm(h, w_in, l, tm=1024, tn=IN_TILE, out_dtype=F32, col_tiles=KV_TILES, name="mm_in_kv")
        rest = _mm(h, w_in, l, tm=1024, tn=IN_TILE, out_dtype=BF16, col_tiles=REST_TILES, name="mm_in")
        if l + 1 < DEPTH:
            gates, ada_next = _mm(h, w_gate, l, tm=1024, tn=1024, out_dtype=BF16, bias=b_gate,
                                  ada=(cs, w_ada, b_ada, l + 1), name="mm_gate")
            mod_next = _mod_vectors(ada_next)
        else:
            gates = _mm(h, w_gate, l, tm=1024, tn=1024, out_dtype=BF16, bias=b_gate, name="mm_gate")

        branch_args = (rest, conv_a_w, conv_a_b, ln_a_g, ln_a_b, conv_c_w, dft, l)
        ctx = _branches(*branch_args, SEQ, BATCH, 0)
        lat = _branches(*branch_args, DEC_SEQ, DEC_BATCH, P_ROWS)
        ya, yc, yd = zip(ctx, lat)
        lam_args = (lam_q1, lam_k1, lam_q2, lam_k2, subln_g, l)
        yb_ctx, new_k, new_v = _attn_ctx(rest, kv, *lam_args, new_k, new_v)
        yb = (yb_ctx, _attn_lat(rest, kv, cache_k, cache_v, rope, *lam_args))

        merged = _merge((ya, yb, yc, yd), gates, (w_a_out, w_b_out, w_c_out, w_d_out), l)
        x, h2 = _mm_resid(merged, w_o, x, mod, l, 2, g_post_mix, (g_pre_ffn, l, 4, 3, mod),
                          tm=512, piece=128, chunk=512, name="mm_o")
        act = _ffn_up(h2, w_ffn_up, l)
        nxt = (g_pre_mix, l + 1, 1, 0, mod_next) if l + 1 < DEPTH else None
        res = _mm_resid(act, w_ffn_down, x, mod, l, 5, g_post_ffn, nxt,
                        tm=256, piece=128, chunk=512, name="mm_down")
        if nxt is not None:
            x, h = res
            mod = mod_next

    y_ctx, y_lat = res
    return y_ctx.reshape(BATCH, SEQ, D_MODEL), y_lat.reshape(DEC_BATCH, DEC_SEQ, D_MODEL), new_k, new_v
```

```python
import functools
import math

import numpy as np
import jax
import jax.numpy as jnp
from jax import lax
from jax.experimental import pallas as pl
from jax.experimental.pallas import tpu as pltpu

D_MODEL = 2048
BATCH = 16
SEQ = 256
DEPTH = 4
DEC_BATCH = 2
DEC_SEQ = 1024
PAST_LEN = 512
GRID_W = 64
N_HEADS = 4
HEAD_DIM = 128
V_DIM = 2 * HEAD_DIM
ATTN_W = N_HEADS * 2 * HEAD_DIM
CONV_W = 512
CONV_K = 31
SC_W = 512
SC_K = 3
F_W = 512
F_GROUPS = 4
F_GROUP = F_W // F_GROUPS
N_BRANCH = 4
D_IN = 2 * CONV_W + 3 * ATTN_W + 3 * SC_W + F_W
D_FF = ((8 * D_MODEL + 3 * 256 - 1) // (3 * 256)) * 256
ROPE_THETA = 10000.0
EPS = 1e-6

P_ROWS = BATCH * SEQ
S_ROWS = DEC_BATCH * DEC_SEQ
N_ROWS = P_ROWS + S_ROWS
N_GROUPS = 1 + DEC_BATCH
N_MOD = 6
ADA_ROWS = 8

IN_TILE = 1024
KV_TILES = (2, 3)
REST_TILES = (0, 1, 4, 5)
KV_K = 0
KV_V = ATTN_W
COL_A = 0
COL_Q = 2 * CONV_W
COL_SB = COL_Q + ATTN_W
COL_SC = COL_SB + SC_W
COL_SX = COL_SC + SC_W
COL_D = COL_SX + SC_W

SUBLANES = 8
CONV_PAD = 16
CONV_ROWS = 64
VMEM_LIMIT = 56 * 1024 * 1024
MM_PIECE = 256

F32 = jnp.float32
BF16 = jnp.bfloat16


def _params(n_axes, vmem=VMEM_LIMIT):
    return pltpu.CompilerParams(dimension_semantics=("arbitrary",) * n_axes, vmem_limit_bytes=vmem)


def _group_of_tile(i, tm):
    pt = P_ROWS // tm
    st = DEC_SEQ // tm
    return jnp.maximum(i - (pt - st), 0) // st


def _mod_spec(which, tm, tile_of):
    def index(*ids):
        return (_group_of_tile(tile_of(*ids), tm) * N_MOD + which, 0, 0)
    return pl.BlockSpec((None, 1, D_MODEL), index)


def _mod_vectors(ada_rows):
    return ada_rows[:N_GROUPS].reshape(N_GROUPS * N_MOD, 1, D_MODEL)


def _layer_vec_spec(width, layer):
    return pl.BlockSpec((None, 1, width), lambda *ids: (layer, 0, 0))


def _ada_chunk(c_ref, w_ref, b_ref, o_ref):
    s = _silu(c_ref[...]).astype(BF16)
    o_ref[...] = jnp.dot(s, w_ref[...].astype(BF16), preferred_element_type=F32) + b_ref[...]


def _ada(cs, w_ada, b_ada, layer):
    tn = 1024
    n = N_MOD * D_MODEL
    return pl.pallas_call(
        _ada_chunk,
        grid=(n // tn,),
        in_specs=[
            pl.BlockSpec((ADA_ROWS, D_MODEL), lambda j: (0, 0)),
            pl.BlockSpec((None, D_MODEL, tn), lambda j: (layer, 0, j)),
            pl.BlockSpec((None, 1, tn), lambda j: (layer, 0, j)),
        ],
        out_specs=pl.BlockSpec((ADA_ROWS, tn), lambda j: (0, j)),
        out_shape=jax.ShapeDtypeStruct((ADA_ROWS, n), F32),
        compiler_params=_params(1),
        name="ada",
    )(cs, w_ada, b_ada)


def _rms(x):
    return x * lax.rsqrt(jnp.mean(x * x, axis=-1, keepdims=True) + EPS)


def _sigmoid(x):
    return 0.5 * jnp.tanh(0.5 * x) + 0.5


def _silu(x):
    return x * _sigmoid(x)


def _normmod_kernel(xc_ref, xl_ref, g_ref, sc_ref, sh_ref, x_ref, h_ref, *, ctx_tiles):
    def tile(src_ref):
        x = src_ref[...]
        x_ref[...] = x
        y = _rms(x) * g_ref[...]
        h_ref[...] = (y * (1.0 + sc_ref[...]) + sh_ref[...]).astype(BF16)

    is_ctx = pl.program_id(0) < ctx_tiles
    pl.when(is_ctx)(lambda: tile(xc_ref))
    pl.when(jnp.logical_not(is_ctx))(lambda: tile(xl_ref))


def _normmod(x_ctx, x_lat, g, mod, layer, which_sc, which_sh):
    tm = 256
    ctx_tiles = P_ROWS // tm
    row = pl.BlockSpec((tm, D_MODEL), lambda i: (i, 0))
    tile = lambda i: i
    return pl.pallas_call(
        functools.partial(_normmod_kernel, ctx_tiles=ctx_tiles),
        grid=(N_ROWS // tm,),
        in_specs=[pl.BlockSpec((tm, D_MODEL), lambda i: (jnp.minimum(i, ctx_tiles - 1), 0)),
                  pl.BlockSpec((tm, D_MODEL), lambda i: (jnp.maximum(i - ctx_tiles, 0), 0)),
                  _layer_vec_spec(D_MODEL, layer),
                  _mod_spec(which_sc, tm, tile), _mod_spec(which_sh, tm, tile)],
        out_specs=[row, row],
        out_shape=[jax.ShapeDtypeStruct((N_ROWS, D_MODEL), F32), jax.ShapeDtypeStruct((N_ROWS, D_MODEL), BF16)],
        compiler_params=_params(1),
        name="normmod",
    )(x_ctx, x_lat, g, mod, mod)


def _cast_weight(w_ref, wbf_ref):
    rows = w_ref.shape[0]
    chunk = 256

    def body(r, carry):
        r0 = pl.multiple_of(r * chunk, chunk)
        wbf_ref[pl.ds(r0, chunk), :] = w_ref[pl.ds(r0, chunk), :].astype(BF16)
        return carry

    lax.fori_loop(0, rows // chunk, body, 0)


def _mm_pieces(tm, tn, split_rows):
    cols = [slice(c, c + MM_PIECE) for c in range(0, tn, MM_PIECE)]
    if not split_rows:
        return [(slice(0, tm), c) for c in cols]
    pieces = [(slice(0, tm), c) for c in cols[:-2]]
    for rows in (slice(0, tm // 2), slice(tm // 2, tm)):
        pieces += [(rows, c) for c in cols[-2:]]
    return pieces


def _mm_kernel(x_ref, w_ref, *rest, sigmoid_bias, with_ada):
    rest = list(rest)
    b_ref = rest.pop(0) if sigmoid_bias else None
    if with_ada:
        c_ref, wa_ref, ba_ref, o_ref, mod_ref, wbf_ref = rest
    else:
        o_ref, wbf_ref = rest

    @pl.when(pl.program_id(1) == 0)
    def _():
        _cast_weight(w_ref, wbf_ref)

    x = x_ref[...]
    for n, (rows, cols) in enumerate(_mm_pieces(*o_ref.shape, split_rows=False)):
        acc = jnp.dot(x, wbf_ref[:, cols], preferred_element_type=F32)
        if sigmoid_bias:
            acc = _sigmoid(acc + b_ref[:, cols])
        o_ref[rows, cols] = acc.astype(o_ref.dtype)
        if with_ada and n == 0:
            _ada_chunk(c_ref, wa_ref, ba_ref, mod_ref)


def _mm(x, w, layer, *, tm, tn, out_dtype, col_tiles=None, bias=None, ada=None, name):
    m, k = x.shape
    if col_tiles is None:
        col_tiles = tuple(range(w.shape[2] // tn))
    n = len(col_tiles) * tn
    steps = (n // tn) * (m // tm)

    def wcol(j):
        return sum((j == idx) * tile for idx, tile in enumerate(col_tiles))

    in_specs = [pl.BlockSpec((tm, k), lambda j, i: (i, 0)),
                pl.BlockSpec((None, k, tn), lambda j, i: (layer, 0, wcol(j)))]
    args = [x, w]
    out_specs = [pl.BlockSpec((tm, tn), lambda j, i: (i, j))]
    out_shape = [jax.ShapeDtypeStruct((m, n), out_dtype)]
    if bias is not None:
        in_specs.append(pl.BlockSpec((None, 1, tn), lambda j, i: (layer, 0, wcol(j))))
        args.append(bias)
    if ada is not None:
        cs, w_ada, b_ada, layer_a = ada
        n_ada = w_ada.shape[2]
        ta = n_ada // steps
        assert ta * steps == n_ada and ta % 128 == 0
        step = lambda j, i: j * (m // tm) + i
        in_specs += [pl.BlockSpec((ADA_ROWS, D_MODEL), lambda j, i: (0, 0)),
                     pl.BlockSpec((None, D_MODEL, ta), lambda j, i: (layer_a, 0, step(j, i))),
                     pl.BlockSpec((None, 1, ta), lambda j, i: (layer_a, 0, step(j, i)))]
        args += [cs, w_ada, b_ada]
        out_specs.append(pl.BlockSpec((ADA_ROWS, ta), lambda j, i: (0, step(j, i))))
        out_shape.append(jax.ShapeDtypeStruct((ADA_ROWS, n_ada), F32))
    res = pl.pallas_call(
        functools.partial(_mm_kernel, sigmoid_bias=bias is not None, with_ada=ada is not None),
        grid=(n // tn, m // tm),
        in_specs=in_specs,
        out_specs=out_specs,
        out_shape=out_shape,
        scratch_shapes=[pltpu.VMEM((k, tn), BF16)],
        compiler_params=_params(2),
        name=name,
    )(*args)
    return res if ada is not None else res[0]


def _merge_kernel(*refs, ctx_tiles):
    ctx_branches, lat_branches = refs[0:N_BRANCH], refs[N_BRANCH:2 * N_BRANCH]
    gate_refs = refs[2 * N_BRANCH:3 * N_BRANCH]
    w_refs = refs[3 * N_BRANCH:4 * N_BRANCH]
    o_ref = refs[4 * N_BRANCH]
    w_scratch = refs[4 * N_BRANCH + 1:]

    @pl.when(pl.program_id(1) == 0)
    def _():
        for w_ref, w_s in zip(w_refs, w_scratch):
            _cast_weight(w_ref, w_s)

    def merge(branches):
        acc = None
        for y_ref, g_ref, w_s in zip(branches, gate_refs, w_scratch):
            term = g_ref[...].astype(F32) * jnp.dot(y_ref[...], w_s[...], preferred_element_type=F32)
            acc = term if acc is None else acc + term
        o_ref[...] = acc.astype(BF16)

    is_ctx = pl.program_id(1) < ctx_tiles
    pl.when(is_ctx)(lambda: merge(ctx_branches))
    pl.when(jnp.logical_not(is_ctx))(lambda: merge(lat_branches))


def _merge(branches, gates, weights, layer):
    tm, tn = 512, 1024
    nt = D_MODEL // tn
    ctx_tiles = P_ROWS // tm
    widths = [w.shape[1] for w in weights]

    def ctx(width):
        return pl.BlockSpec((tm, width), lambda j, i: (jnp.minimum(i, ctx_tiles - 1), 0))

    def lat(width):
        return pl.BlockSpec((tm, width), lambda j, i: (jnp.maximum(i - ctx_tiles, 0), 0))

    def gate(branch):
        return pl.BlockSpec((tm, tn), lambda j, i: (i, branch * nt + j))

    def wt(width):
        return pl.BlockSpec((None, width, tn), lambda j, i: (layer, 0, j))

    return pl.pallas_call(
        functools.partial(_merge_kernel, ctx_tiles=ctx_tiles),
        grid=(nt, N_ROWS // tm),
        in_specs=([ctx(w) for w in widths] + [lat(w) for w in widths]
                  + [gate(b) for b in range(N_BRANCH)] + [wt(w) for w in widths]),
        out_specs=pl.BlockSpec((tm, tn), lambda j, i: (i, j)),
        out_shape=jax.ShapeDtypeStruct((N_ROWS, D_MODEL), BF16),
        scratch_shapes=[pltpu.VMEM((w, tn), BF16) for w in widths],
        compiler_params=_params(2),
        name="merge",
    )(*[b[0] for b in branches], *[b[1] for b in branches], *([gates] * N_BRANCH), *weights)


def _ffn_up_kernel(x_ref, wu_ref, wg_ref, o_ref, wu_s, wg_s):
    @pl.when(pl.program_id(1) == 0)
    def _():
        _cast_weight(wu_ref, wu_s)
        _cast_weight(wg_ref, wg_s)

    for rows, cols in _mm_pieces(*o_ref.shape, split_rows=True):
        x = x_ref[rows, :]
        up = jnp.dot(x, wu_s[:, cols], preferred_element_type=F32)
        gt = jnp.dot(x, wg_s[:, cols], preferred_element_type=F32)
        o_ref[rows, cols] = (_silu(gt) * up).astype(BF16)


def _ffn_up(h, w_up, layer):
    tm, tn = 2048, 512
    nt = D_FF // tn
    return pl.pallas_call(
        _ffn_up_kernel,
        grid=(nt, N_ROWS // tm),
        in_specs=[pl.BlockSpec((tm, D_MODEL), lambda j, i: (i, 0)),
                  pl.BlockSpec((None, D_MODEL, tn), lambda j, i: (layer, 0, j)),
                  pl.BlockSpec((None, D_MODEL, tn), lambda j, i: (layer, 0, nt + j))],
        out_specs=pl.BlockSpec((tm, tn), lambda j, i: (i, j)),
        out_shape=jax.ShapeDtypeStruct((N_ROWS, D_FF), BF16),
        scratch_shapes=[pltpu.VMEM((D_MODEL, tn), BF16), pltpu.VMEM((D_MODEL, tn), BF16)],
        compiler_params=_params(2),
        name="ffn_up",
    )(h, w_up, w_up)


EPI_ROWS = 16


def _mm_resid_kernel(*refs, nw, chunk, tm, piece, with_next):
    if with_next:
        a_ref, w_ref, x_ref, gate_ref, gpost_ref, gpre_ref, sc_ref, sh_ref, xo_ref, h_ref, wbf_ref = refs
    else:
        a_ref, w_ref, x_ref, gate_ref, gpost_ref, yctx_ref, ylat_ref, wbf_ref = refs
    s = pl.program_id(0)

    @pl.when(s < nw)
    def _():
        r0 = pl.multiple_of(s * chunk, chunk)
        wbf_ref[pl.ds(r0, chunk), :] = w_ref[...].astype(BF16)

    def row_tile(xo_ref):
        gg = gate_ref[...] * gpost_ref[...]
        if with_next:
            pp = gpre_ref[...] * (1.0 + sc_ref[...])
            sh = sh_ref[...]
        for p0 in range(0, tm, piece):
            f = jnp.dot(a_ref[p0:p0 + piece, :], wbf_ref[...], preferred_element_type=F32)
            for r in range(0, piece, EPI_ROWS):
                rows = slice(p0 + r, p0 + r + EPI_ROWS)
                xn = x_ref[rows, :] + _rms(f[r:r + EPI_ROWS, :]) * gg
                xo_ref[rows, :] = xn
                if with_next:
                    h_ref[rows, :] = (_rms(xn) * pp + sh).astype(BF16)

    if with_next:
        pl.when(s >= nw)(lambda: row_tile(xo_ref))
    else:
        first_lat = nw + P_ROWS // tm
        pl.when((s >= nw) & (s < first_lat))(lambda: row_tile(yctx_ref))
        pl.when(s >= first_lat)(lambda: row_tile(ylat_ref))


def _mm_resid(a, w, x, mod, layer, which_gate, g_post, nxt, *, tm, piece, chunk, name):
    k = a.shape[1]
    nw = k // chunk
    nt = N_ROWS // tm
    ctx_tiles = P_ROWS // tm
    tile = lambda s: jnp.maximum(s - nw, 0)
    row = pl.BlockSpec((tm, D_MODEL), lambda s: (tile(s), 0))
    in_specs = [pl.BlockSpec((tm, k), lambda s: (tile(s), 0)),
                pl.BlockSpec((None, chunk, D_MODEL), lambda s: (layer, jnp.minimum(s, nw - 1), 0)),
                row, _mod_spec(which_gate, tm, tile), _layer_vec_spec(D_MODEL, layer)]
    args = [a, w, x, mod, g_post]
    if nxt is not None:
        g_pre, nl, wsc, wsh, mod_next = nxt
        in_specs += [_layer_vec_spec(D_MODEL, nl), _mod_spec(wsc, tm, tile), _mod_spec(wsh, tm, tile)]
        args += [g_pre, mod_next, mod_next]
        out_specs = [row, row]
        out_shape = [jax.ShapeDtypeStruct((N_ROWS, D_MODEL), F32), jax.ShapeDtypeStruct((N_ROWS, D_MODEL), BF16)]
    else:
        out_specs = [pl.BlockSpec((tm, D_MODEL), lambda s: (jnp.minimum(tile(s), ctx_tiles - 1), 0)),
                     pl.BlockSpec((tm, D_MODEL), lambda s: (jnp.maximum(tile(s) - ctx_tiles, 0), 0))]
        out_shape = [jax.ShapeDtypeStruct((P_ROWS, D_MODEL), F32), jax.ShapeDtypeStruct((S_ROWS, D_MODEL), F32)]
    return pl.pallas_call(
        functools.partial(_mm_resid_kernel, nw=nw, chunk=chunk, tm=tm, piece=piece, with_next=nxt is not None),
        grid=(nw + nt,),
        in_specs=in_specs,
        out_specs=out_specs,
        out_shape=out_shape,
        scratch_shapes=[pltpu.VMEM((k, D_MODEL), BF16)],
        compiler_params=_params(1),
        name=name,
    )(*args)


SEQ_BLOCK = DEC_SEQ


def _seq_spec(rows, width, col, row0):
    return pl.BlockSpec((rows, width), lambda b: (row0 // rows + b, col // width))


def _dwconv_chunk(pad_ref, w_ref, r0, ksize):
    win = CONV_ROWS + 2 * CONV_PAD
    window = pad_ref[r0:r0 + win, :]
    acc = None
    for b in range(SUBLANES):
        taps = [j for j in range(ksize) if (CONV_PAD - ksize // 2 + j) % SUBLANES == b]
        if not taps:
            continue
        shifted = window if b == 0 else pltpu.roll(window, win - b, 0)
        for j in taps:
            a0 = (CONV_PAD - ksize // 2 + j) - b
            term = w_ref[j:j + 1, :] * shifted[a0:a0 + CONV_ROWS, :]
            acc = term if acc is None else acc + term
    return acc


def _fill_pad(pad_ref, u, t):
    zeros = jnp.zeros((CONV_PAD, pad_ref.shape[1]), F32)
    pad_ref[0:CONV_PAD, :] = zeros
    pad_ref[CONV_PAD + t:2 * CONV_PAD + t, :] = zeros
    pad_ref[CONV_PAD:CONV_PAD + t, :] = u


def _conv_a_kernel(a_ref, w_ref, cb_ref, lg_ref, lb_ref, o_ref, pad_ref, *, t):
    for base in range(0, SEQ_BLOCK, t):
        seq = slice(base, base + t)
        a_val = a_ref[seq, 0:CONV_W].astype(F32)
        _fill_pad(pad_ref, a_val * _sigmoid(a_ref[seq, CONV_W:2 * CONV_W].astype(F32)), t)
        for r0 in range(0, t, CONV_ROWS):
            acc = _dwconv_chunk(pad_ref, w_ref, r0, CONV_K) + cb_ref[...]
            mu = jnp.mean(acc, axis=-1, keepdims=True)
            xc = acc - mu
            y = xc * lax.rsqrt(jnp.mean(xc * xc, axis=-1, keepdims=True) + EPS)
            y = y * lg_ref[...] + lb_ref[...]
            o_ref[base + r0:base + r0 + CONV_ROWS, :] = _silu(y).astype(BF16)


def _conv_c_kernel(sb_ref, sc_ref, sx_ref, w_ref, o_ref, pad_ref, *, t):
    for base in range(0, SEQ_BLOCK, t):
        seq = slice(base, base + t)
        _fill_pad(pad_ref, sc_ref[seq, :].astype(F32) * sx_ref[seq, :].astype(F32), t)
        for r0 in range(0, t, CONV_ROWS):
            rows = slice(base + r0, base + r0 + CONV_ROWS)
            gated = sb_ref[rows, :].astype(F32) * _dwconv_chunk(pad_ref, w_ref, r0, SC_K)
            o_ref[rows, :] = gated.astype(BF16)


def _fourier_kernel(u_ref, cc_ref, sc_ref, ct_ref, st_ref, o_ref):
    t = ct_ref.shape[0]
    for base in range(0, SEQ_BLOCK, t):
        seq = slice(base, base + t)
        u = u_ref[seq, :].astype(BF16)
        yc = jnp.dot(u, cc_ref[...], preferred_element_type=F32).astype(BF16)
        ys = jnp.dot(u, sc_ref[...], preferred_element_type=F32).astype(BF16)
        f = (jnp.dot(ct_ref[...], yc, preferred_element_type=F32)
             - jnp.dot(st_ref[...], ys, preferred_element_type=F32))
        o_ref[seq, :] = f.astype(BF16)


def _dft_tables(n):
    idx = np.arange(n)
    ang = 2.0 * np.pi * ((idx[:, None] * idx[None, :]) % n) / n
    s = 1.0 / math.sqrt(n)
    return (np.cos(ang) * s).astype(np.float32), (np.sin(ang) * s).astype(np.float32)


def _fourier_tables():
    cg, sg = _dft_tables(F_GROUP)
    eye = np.eye(F_GROUPS, dtype=np.float32)
    tabs = {"chan": (np.kron(eye, cg), np.kron(eye, sg)), SEQ: _dft_tables(SEQ), DEC_SEQ: _dft_tables(DEC_SEQ)}
    return {k: tuple(jnp.asarray(m).astype(BF16) for m in v) for k, v in tabs.items()}


def _branches_kernel(a_ref, sb_ref, sc_ref, sx_ref, d_ref, wa_ref, cb_ref, lg_ref, lb_ref, wc_ref,
                     cc_ref, cs_ref, ct_ref, st_ref, ya_ref, yc_ref, yd_ref, pad_a, pad_c, *, t):
    _conv_a_kernel(a_ref, wa_ref, cb_ref, lg_ref, lb_ref, ya_ref, pad_a, t=t)
    _conv_c_kernel(sb_ref, sc_ref, sx_ref, wc_ref, yc_ref, pad_c, t=t)
    _fourier_kernel(d_ref, cc_ref, cs_ref, ct_ref, st_ref, yd_ref)


def _branches(rest, conv_a_w, conv_a_b, ln_a_g, ln_a_b, conv_c_w, tabs, layer, t, nseq, row0):
    full = lambda n: pl.BlockSpec((n, n), lambda b: (0, 0))
    lv = _layer_vec_spec(CONV_W, layer)
    in_specs = [_seq_spec(SEQ_BLOCK, 2 * CONV_W, COL_A, row0),
                _seq_spec(SEQ_BLOCK, SC_W, COL_SB, row0), _seq_spec(SEQ_BLOCK, SC_W, COL_SC, row0),
                _seq_spec(SEQ_BLOCK, SC_W, COL_SX, row0), _seq_spec(SEQ_BLOCK, F_W, COL_D, row0),
                pl.BlockSpec((None, CONV_K, CONV_W), lambda b: (layer, 0, 0)), lv, lv, lv,
                pl.BlockSpec((None, SC_K, SC_W), lambda b: (layer, 0, 0)),
                full(F_W), full(F_W), full(t), full(t)]
    out = pl.BlockSpec((SEQ_BLOCK, CONV_W), lambda b: (b, 0))
    out_shape = jax.ShapeDtypeStruct((nseq * t, CONV_W), BF16)
    return pl.pallas_call(
        functools.partial(_branches_kernel, t=t),
        grid=(nseq * t // SEQ_BLOCK,),
        in_specs=in_specs,
        out_specs=[out, out, out],
        out_shape=[out_shape, out_shape, out_shape],
        scratch_shapes=[pltpu.VMEM((t + 2 * CONV_PAD, CONV_W), F32), pltpu.VMEM((t + 2 * CONV_PAD, SC_W), F32)],
        compiler_params=_params(1),
        name="branches",
    )(rest, rest, rest, rest, rest, conv_a_w, conv_a_b, ln_a_g, ln_a_b, conv_c_w, *tabs["chan"], *tabs[t])


def _lam_of(lq1_ref, lk1_ref, lq2_ref, lk2_ref, lam_init):
    s1 = jnp.sum(lq1_ref[...] * lk1_ref[...], axis=-1, keepdims=True)
    s2 = jnp.sum(lq2_ref[...] * lk2_ref[...], axis=-1, keepdims=True)
    return jnp.exp(s1) - jnp.exp(s2) + lam_init


LOGIT_SCALE = HEAD_DIM ** -0.5 * math.log2(math.e)


def _exp2_and_rsum(t):
    e = jnp.exp2(t - jnp.max(t, axis=-1, keepdims=True))
    return e, 1.0 / jnp.sum(e, axis=-1, keepdims=True)


def _transposed_chunks(k):
    return [k[:, c * HEAD_DIM:(c + 1) * HEAD_DIM].T.astype(BF16) for c in range(ATTN_W // HEAD_DIM)]


def _diff_attn_heads(q, kt_of, v_of, lam, subln, lam_init, o_ref):
    def logits(h):
        maps = []
        for m in range(2):
            c = 2 * h + m
            qc = (q[:, c * HEAD_DIM:(c + 1) * HEAD_DIM] * LOGIT_SCALE).astype(BF16)
            maps.append(jnp.dot(qc, kt_of(c), preferred_element_type=F32))
        return maps

    pending = logits(0)
    for h in range(N_HEADS):
        following = logits(h + 1) if h + 1 < N_HEADS else None
        (e1, r1), (e2, r2) = (_exp2_and_rsum(t) for t in pending)
        a = (e1 * r1 - e2 * (lam * r2)).astype(BF16)
        o = jnp.dot(a, v_of(h), preferred_element_type=F32)
        o = _rms(o) * subln * (1.0 - lam_init)
        o_ref[:, h * V_DIM:(h + 1) * V_DIM] = o.astype(BF16)
        pending = following


def _attn_ctx_kernel(q_ref, k_ref, v_ref, lq1, lk1, lq2, lk2, sg_ref, nk_in, nv_in, o_ref, nk_ref, nv_ref, *,
                     lam_init):
    lam = _lam_of(lq1, lk1, lq2, lk2, lam_init)
    k = k_ref[...]
    v = v_ref[...]
    for h in range(N_HEADS):
        nk_ref[:, h, :] = k[:, h * V_DIM:(h + 1) * V_DIM]
        nv_ref[:, h, :] = v[:, h * V_DIM:(h + 1) * V_DIM]
    kt =_transposed_chunks(k)
    vb = v.astype(BF16)
    _diff_attn_heads(q_ref[...].astype(F32),
                     lambda c: kt[c],
                     lambda h: vb[:, h * V_DIM:(h + 1) * V_DIM],
                     lam, sg_ref[...], lam_init, o_ref)


def _attn_ctx(rest, kv, lam_q1, lam_k1, lam_q2, lam_k2, subln_g, layer, new_k, new_v):
    lam_init = 0.8 - 0.6 * math.exp(-0.3 * layer)
    lv = _layer_vec_spec(HEAD_DIM, layer)
    hbm = pl.BlockSpec(memory_space=pl.ANY)
    in_specs = [_seq_spec(SEQ, ATTN_W, COL_Q, 0), _seq_spec(SEQ, ATTN_W, KV_K, 0),
                _seq_spec(SEQ, ATTN_W, KV_V, 0), lv, lv, lv, lv, _layer_vec_spec(V_DIM, layer), hbm, hbm]
    args = [rest, kv, kv, lam_q1, lam_k1, lam_q2, lam_k2, subln_g, new_k, new_v]
    kv_spec = pl.BlockSpec((None, None, SEQ, N_HEADS, V_DIM), lambda b: (b, layer, 0, 0, 0))
    kv_shape = jax.ShapeDtypeStruct(new_k.shape, F32)
    return pl.pallas_call(
        functools.partial(_attn_ctx_kernel, lam_init=lam_init),
        grid=(BATCH,),
        in_specs=in_specs,
        out_specs=[pl.BlockSpec((SEQ, ATTN_W), lambda b: (b, 0)), kv_spec, kv_spec],
        out_shape=[jax.ShapeDtypeStruct((P_ROWS, ATTN_W), BF16), kv_shape, kv_shape],
        input_output_aliases={len(args) - 2: 1, len(args) - 1: 2},
        compiler_params=_params(1),
        name="attn_ctx",
    )(*args)


def _rope(x, cos, sin_a, sin_b):
    outs = []
    for c in range(ATTN_W // HEAD_DIM):
        xc = x[:, c * HEAD_DIM:(c + 1) * HEAD_DIM]
        fwd = pltpu.roll(xc, HEAD_DIM - HEAD_DIM // 4, 1)
        bwd = pltpu.roll(xc, HEAD_DIM // 4, 1)
        outs.append(xc * cos + fwd * sin_a + bwd * sin_b)
    return outs


def _attn_lat_kernel(q_ref, k_ref, v_ref, ck_ref, cv_ref, cosq_ref, sinaq_ref, sinbq_ref,
                     cos_ref, sina_ref, sinb_ref, lq1, lk1, lq2, lk2, sg_ref, o_ref,
                     kt_ref, vall_ref, *, lam_init):
    @pl.when(pl.program_id(1) == 0)
    def _():
        vall_ref[0:PAST_LEN, :] = cv_ref[...].astype(BF16)
        vall_ref[PAST_LEN:PAST_LEN + DEC_SEQ, :] = v_ref[...].astype(BF16)
        for c, kc in enumerate(_transposed_chunks(ck_ref[...])):
            kt_ref[c * HEAD_DIM:(c + 1) * HEAD_DIM, 0:PAST_LEN] = kc
        for c in range(ATTN_W // HEAD_DIM):
            kt = k_ref[:, c * HEAD_DIM:(c + 1) * HEAD_DIM].T
            fwd = pltpu.roll(kt, HEAD_DIM - HEAD_DIM // 4, 0)
            bwd = pltpu.roll(kt, HEAD_DIM // 4, 0)
            kr = kt * cos_ref[...] + fwd * sina_ref[...] + bwd * sinb_ref[...]
            kt_ref[c * HEAD_DIM:(c + 1) * HEAD_DIM, PAST_LEN:PAST_LEN + DEC_SEQ] = kr.astype(BF16)

    lam = _lam_of(lq1, lk1, lq2, lk2, lam_init)
    q = jnp.concatenate(_rope(q_ref[...].astype(F32), cosq_ref[...], sinaq_ref[...], sinbq_ref[...]), axis=-1)
    _diff_attn_heads(q,
                     lambda c: kt_ref[c * HEAD_DIM:(c + 1) * HEAD_DIM, :],
                     lambda h: vall_ref[:, h * V_DIM:(h + 1) * V_DIM],
                     lam, sg_ref[...], lam_init, o_ref)


def _rope_tables():
    t = jnp.arange(DEC_SEQ)
    row = (t // GRID_W).astype(F32)
    col = (t % GRID_W).astype(F32)
    half = HEAD_DIM // 2
    inv = ROPE_THETA ** (-jnp.arange(0, half, 2, dtype=F32) / half)
    ar = row[:, None] * inv
    ac = col[:, None] * inv
    ang = jnp.concatenate([ar, ar, ac, ac], axis=-1)
    cos, sin = jnp.cos(ang), jnp.sin(ang)
    first = (np.arange(HEAD_DIM) % half) < (half // 2)
    sin_a = jnp.where(first, -sin, 0.0)
    sin_b = jnp.where(first, 0.0, sin)
    return cos, sin_a, sin_b, cos.T, sin_a.T, sin_b.T


def _attn_lat(rest, kv_proj, cache_k, cache_v, rope, lam_q1, lam_k1, lam_q2, lam_k2, subln_g, layer):
    lam_init = 0.8 - 0.6 * math.exp(-0.3 * layer)
    tq = 512
    nq = DEC_SEQ // tq
    lv = _layer_vec_spec(HEAD_DIM, layer)
    qtab = pl.BlockSpec((tq, HEAD_DIM), lambda b, i: (i, 0))
    ktab = pl.BlockSpec((HEAD_DIM, DEC_SEQ), lambda b, i: (0, 0))
    cache = pl.BlockSpec((None, None, PAST_LEN, ATTN_W), lambda b, i: (b, layer, 0, 0))

    def kv(col):
        return pl.BlockSpec((DEC_SEQ, ATTN_W), lambda b, i: (P_ROWS // DEC_SEQ + b, col // ATTN_W))

    args = [rest, kv_proj, kv_proj, cache_k, cache_v, *rope, lam_q1, lam_k1, lam_q2, lam_k2, subln_g]
    return pl.pallas_call(
        functools.partial(_attn_lat_kernel, lam_init=lam_init),
        grid=(DEC_BATCH, nq),
        in_specs=[pl.BlockSpec((tq, ATTN_W), lambda b, i: (P_ROWS // tq + b * nq + i, COL_Q // ATTN_W)),
                  kv(KV_K), kv(KV_V), cache, cache, qtab, qtab, qtab, ktab, ktab, ktab,
                  lv, lv, lv, lv, _layer_vec_spec(V_DIM, layer)],
        out_specs=pl.BlockSpec((tq, ATTN_W), lambda b, i: (b * nq + i, 0)),
        out_shape=jax.ShapeDtypeStruct((S_ROWS, ATTN_W), BF16),
        scratch_shapes=[pltpu.VMEM((ATTN_W, PAST_LEN + DEC_SEQ), BF16),
                        pltpu.VMEM((PAST_LEN + DEC_SEQ, ATTN_W), BF16)],
        compiler_params=_params(2),
        name="attn_lat",
    )(*args)


def kernel(x_prompt, x_sample, cache_k, cache_v, c, c_ctx, w_ada, b_ada, g_pre_mix, g_post_mix,
           g_pre_ffn, g_post_ffn, w_in, conv_a_w, conv_a_b, ln_a_g, ln_a_b, w_a_out,
           lam_q1, lam_k1, lam_q2, lam_k2, subln_g, w_b_out, conv_c_w, w_c_out, w_d_out,
           w_gate, b_gate, w_o, w_ffn_up, w_ffn_down):
    cs = jnp.concatenate([c_ctx[None, :], c, jnp.zeros((ADA_ROWS - N_GROUPS, D_MODEL), F32)], axis=0)

    vec = lambda a: a.reshape(DEPTH, 1, a.shape[-1])
    g_pre_mix, g_post_mix, g_pre_ffn, g_post_ffn = map(vec, (g_pre_mix, g_post_mix, g_pre_ffn, g_post_ffn))
    conv_a_b, ln_a_g, ln_a_b, subln_g, b_gate, b_ada = map(vec, (conv_a_b, ln_a_g, ln_a_b, subln_g, b_gate, b_ada))
    mod = _mod_vectors(_ada(cs, w_ada, b_ada, 0))
    lam_q1, lam_k1, lam_q2, lam_k2 = map(vec, (lam_q1, lam_k1, lam_q2, lam_k2))
    cache_k = cache_k.reshape(DEC_BATCH, DEPTH, PAST_LEN, ATTN_W)
    cache_v = cache_v.reshape(DEC_BATCH, DEPTH, PAST_LEN, ATTN_W)
    rope = _rope_tables()
    dft = _fourier_tables()

    x, h = _normmod(x_prompt.reshape(P_ROWS, D_MODEL), x_sample.reshape(S_ROWS, D_MODEL),
                    g_pre_mix, mod, 0, 1, 0)
    new_k = jnp.zeros((BATCH, DEPTH, SEQ, N_HEADS, V_DIM), F32)
    new_v = jnp.zeros((BATCH, DEPTH, SEQ, N_HEADS, V_DIM), F32)
    for l in range(DEPTH):
        kv = _mm(h, w_in, l, tm=1536, tn=IN_TILE, out_dtype=F32, col_tiles=KV_TILES, name="mm_in_kv")
        rest = _mm(h, w_in, l, tm=1536, tn=IN_TILE, out_dtype=BF16, col_tiles=REST_TILES, name="mm_in")
        if l + 1 < DEPTH:
            gates, ada_next = _mm(h, w_gate, l, tm=1536, tn=1024, out_dtype=BF16, bias=b_gate,
                                  ada=(cs, w_ada, b_ada, l + 1), name="mm_gate")
            mod_next = _mod_vectors(ada_next)
        else:
            gates = _mm(h, w_gate, l, tm=1536, tn=1024, out_dtype=BF16, bias=b_gate, name="mm_gate")

        branch_args = (rest, conv_a_w, conv_a_b, ln_a_g, ln_a_b, conv_c_w, dft, l)
        ctx = _branches(*branch_args, SEQ, BATCH, 0)
        lat = _branches(*branch_args, DEC_SEQ, DEC_BATCH, P_ROWS)
        ya, yc, yd = zip(ctx, lat)
        lam_args = (lam_q1, lam_k1, lam_q2, lam_k2, subln_g, l)
        yb_ctx, new_k, new_v = _attn_ctx(rest, kv, *lam_args, new_k, new_v)
        yb = (yb_ctx, _attn_lat(rest, kv, cache_k, cache_v, rope, *lam_args))

        merged = _merge((ya, yb, yc, yd), gates, (w_a_out, w_b_out, w_c_out, w_d_out), l)
        x, h2 = _mm_resid(merged, w_o, x, mod, l, 2, g_post_mix, (g_pre_ffn, l, 4, 3, mod),
                          tm=512, piece=128, chunk=512, name="mm_o")
        act = _ffn_up(h2, w_ffn_up, l)
        nxt = (g_pre_mix, l + 1, 1, 0, mod_next) if l + 1 < DEPTH else None
        res = _mm_resid(act, w_ffn_down, x, mod, l, 5, g_post_ffn, nxt,
                        tm=256, piece=128, chunk=512, name="mm_down")
        if nxt is not None:
            x, h = res
            mod = mod_next

    y_ctx, y_lat = res
    return y_ctx.reshape(BATCH, SEQ, D_MODEL), y_lat.reshape(DEC_BATCH, DEC_SEQ, D_MODEL), new_k, new_v
```
